```python
import numpy as np
import jax, jax.numpy as jnp
from jax import lax

D_MODEL = 1024
BATCH = 4
SEQ = 4096
DEPTH = 2

GRID_W = 64
CTX_LEN = 256
N_MOD = 9
D_FF = (11 * D_MODEL) // 4
RMS_EPS = 1e-6
POOL_WINDOWS = (2, 4, 8, 16)
N_POOL_GROUPS = 4
POOL_W = D_MODEL // 2
POOL_GW = POOL_W // N_POOL_GROUPS
NA_HEAD_DIM = 64
NA_W = D_MODEL // 2
NA_HEADS = NA_W // NA_HEAD_DIM
NA_SCALE = NA_HEAD_DIM ** -0.5
NB_ROWS = 8
NB_COLS = 16
K_COLS = 2 * NB_COLS
D_RNN = (5 * D_MODEL) // 4
RG_BW = 128
RG_BLOCKS = D_RNN // RG_BW
CONV_W = 4
RG_C = 8.0
N_EVEN = (DEPTH + 1) // 2
N_ODD = DEPTH // 2

kernel_name = "hybrid_pool_natten_rglru_dit_block"

F32 = jnp.float32


def rms_norm(x, g):
    xf = x.astype(F32)
    y = xf * lax.rsqrt(jnp.mean(xf * xf, axis=-1, keepdims=True) + RMS_EPS)
    return y.astype(x.dtype) * g


def modulate(h, shift, scale):
    return h * (1.0 + scale) + shift


def swiglu(h, w_gu, w_down):
    u = h @ w_gu
    return (jax.nn.silu(u[..., :D_FF]) * u[..., D_FF:]) @ w_down


def heads(t):
    return t.reshape(t.shape[0], t.shape[1], NA_HEADS, NA_HEAD_DIM)


def multiscale_pool(u, pool_w, pool_scale):
    B, T, _ = u.shape
    uf = u.astype(F32).reshape(B, T, N_POOL_GROUPS, POOL_GW)
    csum = jnp.pad(jnp.cumsum(uf, axis=1), ((0, 0), (1, 0), (0, 0), (0, 0)))
    t = np.arange(T)[:, None]
    win = np.array(POOL_WINDOWS)[None, :]
    lo = np.clip(t - win // 2, 0, T)
    hi = np.clip(t + win // 2, 0, T)
    gidx = np.arange(N_POOL_GROUPS)
    wsum = csum[:, hi, gidx] - csum[:, lo, gidx]
    cnt = (hi - lo).astype(np.float32)[None, :, :, None]
    y = (wsum / cnt - uf).astype(u.dtype)
    y = jnp.einsum('btgk,gkj->btgj', y, pool_w).reshape(B, T, POOL_W)
    return y * pool_scale


def neighbourhood_attention(q, k, v, kc, vc, rpb):
    B, S, H, Dh = q.shape
    rows = S // GRID_W
    kh = min(NB_ROWS, rows)
    n_cb = GRID_W // NB_COLS
    qg = q.reshape(B, rows, GRID_W, H, Dh)
    kg = k.reshape(B, rows, GRID_W, H, Dh)
    vg = v.reshape(B, rows, GRID_W, H, Dh)
    qcols = np.arange(GRID_W).reshape(n_cb, NB_COLS)
    qstart = np.clip(qcols - NB_COLS // 2, 0, GRID_W - NB_COLS)
    slab_start = np.clip(np.arange(n_cb) * NB_COLS - NB_COLS // 2, 0, GRID_W - K_COLS)
    slab_cols = slab_start[:, None] + np.arange(K_COLS)[None, :]
    col_mask = ((slab_cols[:, None, :] >= qstart[:, :, None])
                & (slab_cols[:, None, :] < qstart[:, :, None] + NB_COLS))
    dc_idx = np.clip(slab_cols[:, None, :] - qcols[:, :, None] + NB_COLS - 1, 0, 2 * NB_COLS - 2)
    rpb_cols = rpb[:, :, dc_idx]
    mask6 = col_mask[None, None, :, :, None, :]
    n_loc = kh * K_COLS

    def row_fn(args):
        r, q_row = args
        r0 = jnp.clip(r - kh // 2, 0, rows - kh)
        k_rows = lax.dynamic_slice_in_dim(kg, r0, kh, axis=1)
        v_rows = lax.dynamic_slice_in_dim(vg, r0, kh, axis=1)
        k_slab = k_rows[:, :, slab_cols]
        v_slab = v_rows[:, :, slab_cols]
        qb = q_row.reshape(B, n_cb, NB_COLS, H, Dh)
        s_loc = jnp.einsum('bjqhd,bkjchd->bhjqkc', qb, k_slab, preferred_element_type=F32) * NA_SCALE
        dr = r0 + jnp.arange(kh) - r + (NB_ROWS - 1)
        bias = jnp.transpose(rpb_cols[:, dr], (0, 2, 3, 1, 4)).astype(F32)
        s_loc = jnp.where(mask6, s_loc + bias[None], -jnp.inf)
        s_ctx = jnp.einsum('bjqhd,bmhd->bhjqm', qb, kc, preferred_element_type=F32) * NA_SCALE
        s = jnp.concatenate([s_loc.reshape(B, H, n_cb, NB_COLS, n_loc), s_ctx], axis=-1)
        p = jax.nn.softmax(s, axis=-1).astype(v.dtype)
        p_loc = p[..., :n_loc].reshape(B, H, n_cb, NB_COLS, kh, K_COLS)
        o = (jnp.einsum('bhjqkc,bkjchd->bjqhd', p_loc, v_slab)
             + jnp.einsum('bhjqm,bmhd->bjqhd', p[..., n_loc:], vc))
        return o.reshape(B, GRID_W, H, Dh)

    out = lax.map(row_fn, (jnp.arange(rows), jnp.moveaxis(qg, 1, 0)))
    return jnp.moveaxis(out, 0, 1).reshape(B, S, H * Dh)


def context_attention(q, k, v):
    B, T = q.shape[0], q.shape[1]
    s = jnp.einsum('bqhd,bkhd->bhqk', q, k, preferred_element_type=F32) * NA_SCALE
    p = jax.nn.softmax(s, axis=-1).astype(v.dtype)
    return jnp.einsum('bhqk,bkhd->bqhd', p, v).reshape(B, T, NA_W)


def pool_nat_mixer(hx, hc, w_in, w_out, pool_w, pool_scale, rpb, need_ctx_out):
    B, S, _ = hx.shape
    ux = hx @ w_in
    px = ux[..., :POOL_W]
    qx = heads(ux[..., POOL_W:POOL_W + NA_W])
    kx = heads(ux[..., POOL_W + NA_W:POOL_W + 2 * NA_W])
    vx = heads(ux[..., POOL_W + 2 * NA_W:])
    kvc = hc @ w_in[:, POOL_W + NA_W:]
    kc = heads(kvc[..., :NA_W])
    vc = heads(kvc[..., NA_W:])
    a_x = multiscale_pool(px, pool_w, pool_scale)
    b_x = neighbourhood_attention(qx, kx, vx, kc, vc, rpb)
    yx = jnp.concatenate([a_x, b_x], axis=-1) @ w_out
    yc = None
    if need_ctx_out:
        uc = hc @ w_in[:, :POOL_W + NA_W]
        a_c = multiscale_pool(uc[..., :POOL_W], pool_w, pool_scale)
        b_c = context_attention(heads(uc[..., POOL_W:]), kc, vc)
        yc = jnp.concatenate([a_c, b_c], axis=-1) @ w_out
    return yx, yc


def centred_dwconv(u, w, b):
    T = u.shape[1]
    left = CONV_W // 2
    up = jnp.pad(u, ((0, 0), (left, CONV_W - 1 - left), (0, 0)))
    y = b
    for k in range(CONV_W):
        y = y + up[:, k:k + T] * w[k]
    return y


def rglru_coeffs(u, gate_w, gate_b, lam):
    B, T, _ = u.shape
    ub = u.reshape(B, T, RG_BLOCKS, RG_BW)
    g = jnp.einsum('btnk,gnkj->gbtnj', ub, gate_w).reshape(2, B, T, D_RNN)
    g = jax.nn.sigmoid(g.astype(F32) + gate_b.astype(F32)[:, None, None, :])
    r, i = g[0], g[1]
    log_a = -RG_C * r * jax.nn.softplus(-lam.astype(F32))
    a = jnp.exp(log_a)
    b = jnp.sqrt(-jnp.expm1(2.0 * log_a)) * (i * u.astype(F32))
    return a, b


def linear_scan(a, b, h0):
    def combine(e1, e2):
        a1, b1 = e1
        a2, b2 = e2
        return a1 * a2, a2 * b1 + b2
    a_cum, b_cum = lax.associative_scan(combine, (a, b), axis=1)
    return a_cum * h0[:, None] + b_cum


def rglru_mixer(hx, hc, w_in, conv_w, conv_b, gate_w, gate_b, lam, w_out, need_ctx_out):
    ux = hx @ w_in
    gx, rx = ux[..., :D_RNN], ux[..., D_RNN:]
    if need_ctx_out:
        uc = hc @ w_in
        gc, rc = uc[..., :D_RNN], uc[..., D_RNN:]
    else:
        rc = hc @ w_in[:, D_RNN:]
    rx = centred_dwconv(rx, conv_w, conv_b)
    rc = centred_dwconv(rc, conv_w, conv_b)
    hs_x, hs_c = [], []
    for d in range(2):
        ac, bc = rglru_coeffs(rc, gate_w[d], gate_b[d], lam[d])
        ax, bx = rglru_coeffs(rx, gate_w[d], gate_b[d], lam[d])
        if d == 1:
            ac, bc, ax, bx = (jnp.flip(t, axis=1) for t in (ac, bc, ax, bx))
        h_c = linear_scan(ac, bc, jnp.zeros_like(bc[:, 0]))
        h_x = linear_scan(ax, bx, h_c[:, -1])
        if d == 1:
            h_c, h_x = jnp.flip(h_c, axis=1), jnp.flip(h_x, axis=1)
        hs_x.append(h_x)
        hs_c.append(h_c)
    yx = (jax.nn.gelu(gx) * (hs_x[0] + hs_x[1]).astype(gx.dtype)) @ w_out
    yc = None
    if need_ctx_out:
        yc = (jax.nn.gelu(gc) * (hs_c[0] + hs_c[1]).astype(gc.dtype)) @ w_out
    return yx, yc


def setup_inputs(seed: int = 0) -> dict:
    key = jax.random.key(seed)
    ks = jax.random.split(key, 24)

    def nrm(k, shape, scale):
        return jax.random.normal(k, shape, F32) * scale

    a_c = jax.random.uniform(ks[19], (N_ODD, 2, D_RNN), F32, 0.9, 0.999)
    a_base = a_c ** (1.0 / RG_C)
    return {
        "x": nrm(ks[0], (BATCH, SEQ, D_MODEL), 1.0),
        "c": nrm(ks[1], (BATCH, D_MODEL), 1.0),
        "ctx": nrm(ks[2], (BATCH, CTX_LEN, D_MODEL), 1.0),
        "c_ctx": nrm(ks[3], (D_MODEL,), 1.0),
        "ada_w": nrm(ks[4], (DEPTH, D_MODEL, N_MOD * D_MODEL), 0.5 * D_MODEL ** -0.5),
        "ada_b": nrm(ks[5], (DEPTH, N_MOD * D_MODEL), 0.01),
        "norm_g": 1.0 + nrm(ks[6], (DEPTH, 3, D_MODEL), 0.02),
        "ffn_w_gu": nrm(ks[7], (DEPTH, 2, D_MODEL, 2 * D_FF), D_MODEL ** -0.5),
        "ffn_w_down": nrm(ks[8], (DEPTH, 2, D_FF, D_MODEL), D_FF ** -0.5),
        "ab_w_in": nrm(ks[9], (N_EVEN, D_MODEL, POOL_W + 3 * NA_W), D_MODEL ** -0.5),
        "ab_w_out": nrm(ks[10], (N_EVEN, POOL_W + NA_W, D_MODEL), (POOL_W + NA_W) ** -0.5),
        "pool_w": nrm(ks[11], (N_EVEN, N_POOL_GROUPS, POOL_GW, POOL_GW), POOL_GW ** -0.5),
        "pool_scale": 1.0 + nrm(ks[12], (N_EVEN, POOL_W), 0.02),
        "na_rpb": nrm(ks[13], (N_EVEN, NA_HEADS, 2 * NB_ROWS - 1, 2 * NB_COLS - 1), 0.02),
        "rg_w_in": nrm(ks[14], (N_ODD, D_MODEL, 2 * D_RNN), D_MODEL ** -0.5),
        "rg_conv_w": nrm(ks[15], (N_ODD, CONV_W, D_RNN), CONV_W ** -0.5),
        "rg_conv_b": nrm(ks[16], (N_ODD, D_RNN), 0.01),
        "rg_gate_w": nrm(ks[17], (N_ODD, 2, 2, RG_BLOCKS, RG_BW, RG_BW), RG_BW ** -0.5),
        "rg_gate_b": nrm(ks[18], (N_ODD, 2, 2, D_RNN), 0.01),
        "rg_lambda": jnp.log(a_base) - jnp.log1p(-a_base),
        "rg_w_out": nrm(ks[20], (N_ODD, D_RNN, D_MODEL), D_RNN ** -0.5),
        "final_g": 1.0 + nrm(ks[21], (D_MODEL,), 0.02),
    }


def reference(x, c, ctx, c_ctx, ada_w, ada_b, norm_g, ffn_w_gu, ffn_w_down, ab_w_in, ab_w_out,
              pool_w, pool_scale, na_rpb, rg_w_in, rg_conv_w, rg_conv_b, rg_gate_w, rg_gate_b,
              rg_lambda, rg_w_out, final_g):
    B = x.shape[0]
    for i in range(DEPTH):
        last = i == DEPTH - 1
        mx = (jax.nn.silu(c) @ ada_w[i] + ada_b[i]).reshape(B, 1, N_MOD, D_MODEL)
        mc = (jax.nn.silu(c_ctx) @ ada_w[i] + ada_b[i]).reshape(N_MOD, D_MODEL)
        x = x + 0.5 * mx[:, :, 2] * swiglu(modulate(rms_norm(x, norm_g[i, 0]), mx[:, :, 0], mx[:, :, 1]),
                                           ffn_w_gu[i, 0], ffn_w_down[i, 0])
        ctx = ctx + 0.5 * mc[2] * swiglu(modulate(rms_norm(ctx, norm_g[i, 0]), mc[0], mc[1]),
                                         ffn_w_gu[i, 0], ffn_w_down[i, 0])
        hx = modulate(rms_norm(x, norm_g[i, 1]), mx[:, :, 3], mx[:, :, 4])
        hc = modulate(rms_norm(ctx, norm_g[i, 1]), mc[3], mc[4])
        j = i // 2
        if i % 2 == 0:
            yx, yc = pool_nat_mixer(hx, hc, ab_w_in[j], ab_w_out[j], pool_w[j], pool_scale[j],
                                    na_rpb[j], not last)
        else:
            yx, yc = rglru_mixer(hx, hc, rg_w_in[j], rg_conv_w[j], rg_conv_b[j], rg_gate_w[j],
                                 rg_gate_b[j], rg_lambda[j], rg_w_out[j], not last)
        x = x + mx[:, :, 5] * yx
        x = x + 0.5 * mx[:, :, 8] * swiglu(modulate(rms_norm(x, norm_g[i, 2]), mx[:, :, 6], mx[:, :, 7]),
                                           ffn_w_gu[i, 1], ffn_w_down[i, 1])
        if not last:
            ctx = ctx + mc[5] * yc
            ctx = ctx + 0.5 * mc[8] * swiglu(modulate(rms_norm(ctx, norm_g[i, 2]), mc[6], mc[7]),
                                             ffn_w_gu[i, 1], ffn_w_down[i, 1])
    return rms_norm(x, final_g)
```

```python
import functools

import numpy as np
import jax
import jax.numpy as jnp
from jax import lax
from jax.experimental import pallas as pl
from jax.experimental.pallas import tpu as pltpu

F32 = jnp.float32
BF16 = jnp.bfloat16

N_MOD = 9
RMS_EPS = 1e-6
GRID_W = 64
POOL_WINDOWS = (2, 4, 8, 16)
NA_HEAD_DIM = 64
NB_ROWS = 8
NB_COLS = 16
RG_BW = 128
CONV_W = 4
RG_C = 8.0

LANES = 128
SUBLANES = 8
MXU_DIM = 256
VMEM_BYTES = 64 * 1024 * 1024

HEADS_PER_GROUP = MXU_DIM // NA_HEAD_DIM
TOKEN_TILE = 512
FF_CHUNK = MXU_DIM


def _cparams(n_axes, vmem_mb):
    return pltpu.CompilerParams(
        dimension_semantics=("arbitrary",) * n_axes,
        vmem_limit_bytes=min(vmem_mb * 1024 * 1024, VMEM_BYTES - 8 * 1024 * 1024),
    )


def _resident(shape):
    nd = len(shape)
    return pl.BlockSpec(shape, lambda *_: (0,) * nd, pipeline_mode=pl.Buffered(1))


def _silu(x):
    return x * jax.nn.sigmoid(x)


def _gelu_tanh(x):
    c = np.sqrt(2.0 / np.pi).astype(np.float32)
    return 0.5 * x * (1.0 + jnp.tanh(c * (x + 0.044715 * (x * x * x))))


def _rms_mod(x, g, shift, scale):
    ms = jnp.mean(x * x, axis=-1, keepdims=True)
    y = (x * lax.rsqrt(ms + RMS_EPS)) * g
    return y * (1.0 + scale) + shift


def _swiglu(h, wgu_ref, wd_ref, act_ref):
    d_ff = wd_ref.shape[0]
    for c in range(d_ff // FF_CHUNK):
        lo = c * FF_CHUNK
        ug = jnp.dot(h, wgu_ref[:, lo:lo + FF_CHUNK], preferred_element_type=F32)
        uu = jnp.dot(h, wgu_ref[:, d_ff + lo:d_ff + lo + FF_CHUNK], preferred_element_type=F32)
        act_ref[:, lo:lo + FF_CHUNK] = (_silu(ug) * uu).astype(BF16)
    return jnp.dot(act_ref[...], wd_ref[...], preferred_element_type=F32)


def _mods_kernel(c_ref, w_ref, b_ref, o_ref):
    h = _silu(c_ref[...]).astype(BF16)
    w = w_ref[0].astype(BF16)
    o_ref[0] = jnp.dot(h, w, preferred_element_type=F32) + b_ref[0]


def _mods(c, c_ctx, ada_w, ada_b):
    depth, d, n = ada_w.shape
    b = c.shape[0]
    rows = jnp.zeros((SUBLANES, d), F32).at[:b].set(c).at[b].set(c_ctx)
    tn = n // 4
    out = pl.pallas_call(
        _mods_kernel,
        grid=(depth, n // tn),
        in_specs=[
            pl.BlockSpec((SUBLANES, d), lambda i, j: (0, 0)),
            pl.BlockSpec((1, d, tn), lambda i, j: (i, 0, j)),
            pl.BlockSpec((1, 1, tn), lambda i, j: (i, 0, j)),
        ],
        out_specs=pl.BlockSpec((1, SUBLANES, tn), lambda i, j: (i, 0, j)),
        out_shape=jax.ShapeDtypeStruct((depth, SUBLANES, n), F32),
        compiler_params=_cparams(2, 40),
        name="mods",
    )(rows, ada_w, ada_b.reshape(depth, 1, n))
    return out.reshape(depth, SUBLANES, N_MOD, d)


def _pre_kernel(x_ref, mod_ref, g_ref, wgu_ref, wd_ref, win_ref, x_out, *rest, splits):
    outs, act_ref = rest[:-1], rest[-1]
    x = x_ref[0]
    h = _rms_mod(x, g_ref[0:1, :], mod_ref[0, 0:1, :], mod_ref[0, 1:2, :]).astype(BF16)
    y = _swiglu(h, wgu_ref, wd_ref, act_ref)
    x = x + (0.5 * mod_ref[0, 2:3, :]) * y
    x_out[0] = x
    h = _rms_mod(x, g_ref[1:2, :], mod_ref[0, 3:4, :], mod_ref[0, 4:5, :]).astype(BF16)
    off = 0
    for o_ref, (width, scale) in zip(outs, splits):
        u = jnp.dot(h, win_ref[:, off:off + width], preferred_element_type=F32)
        if scale != 1.0:
            u = u * scale
        o_ref[0] = u.astype(o_ref.dtype)
        off += width


def _pre(x, mod, g, wgu, wd, win, splits, dtypes, name):
    n, t, d = x.shape
    tm = min(TOKEN_TILE, t)
    tok = lambda w: pl.BlockSpec((1, tm, w), lambda b, i: (b, i, 0))
    out_shape = [jax.ShapeDtypeStruct((n, t, d), F32)]
    out_shape += [jax.ShapeDtypeStruct((n, t, w), dt) for (w, _), dt in zip(splits, dtypes)]
    return pl.pallas_call(
        functools.partial(_pre_kernel, splits=splits),
        grid=(n, t // tm),
        in_specs=[
            tok(d),
            pl.BlockSpec((1, N_MOD, d), lambda b, i: (b, 0, 0)),
            _resident(g.shape), _resident(wgu.shape), _resident(wd.shape), _resident(win.shape),
        ],
        out_specs=[tok(d)] + [tok(w) for w, _ in splits],
        out_shape=out_shape,
        scratch_shapes=[pltpu.VMEM((tm, wd.shape[0]), BF16)],
        compiler_params=_cparams(2, 56),
        name=name,
    )(x, mod, g, wgu, wd, win)


def _post_kernel(x_ref, mod_ref, g_ref, *rest, n_parts, final):
    y_refs = rest[:n_parts]
    wout_ref, wgu_ref, wd_ref = rest[n_parts:n_parts + 3]
    rest = rest[n_parts + 3:]
    if final:
        gf_ref, o_ref, act_ref = rest
    else:
        o_ref, act_ref = rest
    acc = None
    off = 0
    for y_ref in y_refs:
        w = y_ref.shape[-1]
        part = jnp.dot(y_ref[0], wout_ref[off:off + w, :], preferred_element_type=F32)
        acc = part if acc is None else acc + part
        off += w
    x = x_ref[0] + mod_ref[0, 5:6, :] * acc
    h = _rms_mod(x, g_ref[2:3, :], mod_ref[0, 6:7, :], mod_ref[0, 7:8, :]).astype(BF16)
    y = _swiglu(h, wgu_ref, wd_ref, act_ref)
    x = x + (0.5 * mod_ref[0, 8:9, :]) * y
    if final:
        ms = jnp.mean(x * x, axis=-1, keepdims=True)
        x = (x * lax.rsqrt(ms + RMS_EPS)) * gf_ref[...]
    o_ref[0] = x


def _post(x, mod, g, parts, wout, wgu, wd, final_g, name):
    n, t, d = x.shape
    tm = min(TOKEN_TILE, t)
    tok = lambda w: pl.BlockSpec((1, tm, w), lambda b, i: (b, i, 0))
    args = [x, mod, g, *parts, wout, wgu, wd]
    in_specs = [tok(d), pl.BlockSpec((1, N_MOD, d), lambda b, i: (b, 0, 0)), _resident(g.shape)]
    in_specs += [tok(p.shape[-1]) for p in parts]
    in_specs += [_resident(wout.shape), _resident(wgu.shape), _resident(wd.shape)]
    if final_g is not None:
        args.append(final_g)
        in_specs.append(_resident(final_g.shape))
    return pl.pallas_call(
        functools.partial(_post_kernel, n_parts=len(parts), final=final_g is not None),
        grid=(n, t // tm),
        in_specs=in_specs,
        out_specs=tok(d),
        out_shape=jax.ShapeDtypeStruct((n, t, d), F32),
        scratch_shapes=[pltpu.VMEM((tm, wd.shape[0]), BF16)],
        compiler_params=_cparams(2, 56),
        name=name,
    )(*args)


POOL_PAD = SUBLANES * 2


def _pool_kernel(p_ref, w_ref, s_ref, o_ref, pad_ref):
    t = p_ref.shape[1]
    gw = w_ref.shape[-1]
    zeros = jnp.zeros((POOL_PAD, gw), F32)
    pad_ref[0:POOL_PAD, :] = zeros
    pad_ref[POOL_PAD + t:POOL_PAD + t + POOL_PAD, :] = zeros
    pos = lax.broadcasted_iota(jnp.int32, (t, gw), 0)
    for gi, win in enumerate(POOL_WINDOWS):
        half = win // 2
        u = p_ref[0, :, gi * gw:(gi + 1) * gw]
        pad_ref[POOL_PAD:POOL_PAD + t, :] = u
        wsum = None
        for k in range(-half, half):
            piece = pad_ref[POOL_PAD + k:POOL_PAD + k + t, :]
            wsum = piece if wsum is None else wsum + piece
        cnt = (jnp.minimum(pos + half, t) - jnp.maximum(pos - half, 0)).astype(F32)
        y = (wsum / cnt - u).astype(BF16)
        z = jnp.dot(y, w_ref[gi], preferred_element_type=F32)
        o_ref[0, :, gi * gw:(gi + 1) * gw] = (z * s_ref[:, gi * gw:(gi + 1) * gw]).astype(o_ref.dtype)


def _pool(p, pool_w, pool_scale, name):
    n, t, w = p.shape
    gw = pool_w.shape[-1]
    return pl.pallas_call(
        _pool_kernel,
        grid=(n,),
        in_specs=[
            pl.BlockSpec((1, t, w), lambda b: (b, 0, 0)),
            _resident(pool_w.shape), _resident(pool_scale.shape),
        ],
        out_specs=pl.BlockSpec((1, t, w), lambda b: (b, 0, 0)),
        out_shape=jax.ShapeDtypeStruct((n, t, w), BF16),
        scratch_shapes=[pltpu.VMEM((t + 2 * POOL_PAD, gw), F32)],
        compiler_params=_cparams(1, 48),
        name=name,
    )(p, pool_w, pool_scale)


def _head_block_mask(shape):
    r = lax.broadcasted_iota(jnp.int32, shape, 0) // NA_HEAD_DIM
    c = lax.broadcasted_iota(jnp.int32, shape, 1) // NA_HEAD_DIM
    return r == c


def _attend(q_blk, key_parts, val_parts, bias_parts):
    nq, g = q_blk.shape
    heads = g // NA_HEAD_DIM
    qbd = jnp.where(_head_block_mask((heads * nq, g)),
                    jnp.concatenate([q_blk] * heads, axis=0), jnp.zeros((), BF16))
    nt = (((1,), (1,)), ((), ()))
    scores = []
    for keys, bias in zip(key_parts, bias_parts):
        s = lax.dot_general(keys, qbd, nt, preferred_element_type=F32)
        scores.append(s if bias is None else s + bias)
    m = functools.reduce(jnp.maximum, [jnp.max(s, axis=0, keepdims=True) for s in scores])
    ps = [jnp.exp(s - m) for s in scores]
    l = functools.reduce(jnp.add, [jnp.sum(p, axis=0, keepdims=True) for p in ps])
    inv = 1.0 / l
    tn = (((0,), (0,)), ((), ()))
    r = None
    for p, vals in zip(ps, val_parts):
        part = lax.dot_general((p * inv).astype(BF16), vals, tn, preferred_element_type=F32)
        r = part if r is None else r + part
    lane_head = lax.broadcasted_iota(jnp.int32, (nq, g), 1) // NA_HEAD_DIM
    out = jnp.zeros((nq, g), F32)
    for h in range(heads):
        out = jnp.where(lane_head == h, r[h * nq:(h + 1) * nq, :], out)
    return out


def _nattn_kernel(q_ref, k_ref, v_ref, kc_ref, vc_ref, bias_ref, o_ref):
    s = q_ref.shape[1]
    rows = s // GRID_W
    kh = min(NB_ROWS, rows)
    kc = kc_ref[0]
    vc = vc_ref[0]

    def body(i, carry):
        r0 = jnp.clip(i - kh // 2, 0, rows - kh)
        q0 = pl.multiple_of(i * GRID_W, GRID_W)
        k0 = pl.multiple_of(r0 * GRID_W, GRID_W)
        q_blk = q_ref[0, pl.ds(q0, GRID_W), :]
        keys = k_ref[0, pl.ds(k0, kh * GRID_W), :]
        vals = v_ref[0, pl.ds(k0, kh * GRID_W), :]
        bias = bias_ref[0, pl.ds(r0 - i + (NB_ROWS - 1), kh), :, :]
        bias = bias.reshape(kh * GRID_W, bias.shape[-1])
        out = _attend(q_blk, [keys, kc], [vals, vc], [bias, None])
        o_ref[0, pl.ds(q0, GRID_W), :] = out.astype(o_ref.dtype)
        return carry

    lax.fori_loop(0, rows, body, 0)


def _ctx_attn_kernel(q_ref, kc_ref, vc_ref, o_ref):
    t = q_ref.shape[1]
    kc = kc_ref[0]
    vc = vc_ref[0]
    for i in range(t // GRID_W):
        q_blk = q_ref[0, i * GRID_W:(i + 1) * GRID_W, :]
        out = _attend(q_blk, [kc], [vc], [None])
        o_ref[0, i * GRID_W:(i + 1) * GRID_W, :] = out.astype(o_ref.dtype)


def _bias_table(rpb):
    h = rpb.shape[0]
    kcol = np.arange(GRID_W)[:, None]
    qcol = np.arange(GRID_W)[None, :]
    qstart = np.clip(qcol - NB_COLS // 2, 0, GRID_W - NB_COLS)
    mask = (kcol >= qstart) & (kcol < qstart + NB_COLS)
    idx = np.clip(kcol - qcol + NB_COLS - 1, 0, 2 * NB_COLS - 2)
    t = jnp.where(mask[None, None], rpb.astype(F32)[:, :, idx], -jnp.inf)
    t = t.reshape(h // HEADS_PER_GROUP, HEADS_PER_GROUP, t.shape[1], GRID_W, GRID_W)
    t = jnp.transpose(t, (0, 2, 3, 1, 4))
    return t.reshape(h // HEADS_PER_GROUP, t.shape[1], GRID_W, HEADS_PER_GROUP * GRID_W)


def _nattn(q, k, v, kc, vc, rpb):
    b, s, w = q.shape
    ctx = kc.shape[1]
    g = HEADS_PER_GROUP * NA_HEAD_DIM
    bias = _bias_table(rpb)
    seq = lambda t: pl.BlockSpec((1, t, g), lambda i, j: (i, 0, j))
    return pl.pallas_call(
        _nattn_kernel,
        grid=(b, w // g),
        in_specs=[seq(s), seq(s), seq(s), seq(ctx), seq(ctx),
                  pl.BlockSpec((1,) + bias.shape[1:], lambda i, j: (j, 0, 0, 0))],
        out_specs=seq(s),
        out_shape=jax.ShapeDtypeStruct((b, s, w), BF16),
        compiler_params=_cparams(2, 48),
        name="nattn",
    )(q, k, v, kc, vc, bias)


def _ctx_attn(q, kc, vc):
    b, t, w = q.shape
    g = HEADS_PER_GROUP * NA_HEAD_DIM
    seq = pl.BlockSpec((1, t, g), lambda i, j: (i, 0, j))
    return pl.pallas_call(
        _ctx_attn_kernel,
        grid=(b, w // g),
        in_specs=[seq, seq, seq],
        out_specs=seq,
        out_shape=jax.ShapeDtypeStruct((b, t, w), BF16),
        compiler_params=_cparams(2, 32),
        name="ctx_attn",
    )(q, kc, vc)


CONV_PAD = SUBLANES


def _conv_into(src, t, cw_ref, cb_ref, pad_ref, dst_ref, dst0):
    zeros = jnp.zeros((CONV_PAD, src.shape[-1]), F32)
    pad_ref[0:CONV_PAD, :] = zeros
    pad_ref[CONV_PAD + t:CONV_PAD + t + CONV_PAD, :] = zeros
    pad_ref[CONV_PAD:CONV_PAD + t, :] = src
    left = CONV_W // 2
    y = cb_ref[...]
    for k in range(CONV_W):
        y = y + pad_ref[CONV_PAD + k - left:CONV_PAD + k - left + t, :] * cw_ref[k:k + 1, :]
    dst_ref[dst0:dst0 + t, :] = y


def _chunk_scan(a, b, reverse):
    row = lax.broadcasted_iota(jnp.int32, a.shape, 0)
    d = 1
    while d < SUBLANES:
        if reverse:
            keep = row < SUBLANES - d
            shift = SUBLANES - d
        else:
            keep = row >= d
            shift = d
        a_s = jnp.where(keep, pltpu.roll(a, shift, 0), 1.0)
        b_s = jnp.where(keep, pltpu.roll(b, shift, 0), 0.0)
        b = a * b_s + b
        a = a * a_s
        d *= 2
    return a, b


def _rglru_kernel(rx_ref, rc_ref, gx_ref, cw_ref, cb_ref, gw_ref, gb_ref, lam_ref, o_ref,
                  pad_ref, u_ref, af_ref, bf_ref, ab_ref, bb_ref):
    s = rx_ref.shape[1]
    ctx = rc_ref.shape[1]
    c = rx_ref.shape[2]
    total = ctx + s
    _conv_into(rc_ref[0], ctx, cw_ref, cb_ref, pad_ref, u_ref, 0)
    _conv_into(rx_ref[0], s, cw_ref, cb_ref, pad_ref, u_ref, ctx)

    lam = -lam_ref[...]
    softplus = jnp.maximum(lam, 0.0) + jnp.log1p(jnp.exp(-jnp.abs(lam)))
    gw = gw_ref[0]
    gb = gb_ref[0]
    coef_rows = 256

    def coef_body(j, carry):
        r0 = pl.multiple_of(j * coef_rows, coef_rows)
        u = u_ref[pl.ds(r0, coef_rows), :]
        gates = jax.nn.sigmoid(jnp.dot(u.astype(BF16), gw, preferred_element_type=F32) + gb)
        for d, (a_ref, b_ref) in enumerate(((af_ref, bf_ref), (ab_ref, bb_ref))):
            r = gates[:, (2 * d) * c:(2 * d + 1) * c]
            i = gates[:, (2 * d + 1) * c:(2 * d + 2) * c]
            log_a = (-RG_C * r) * softplus[d:d + 1, :]
            a_ref[pl.ds(r0, coef_rows), :] = jnp.exp(log_a)
            th = jnp.tanh(log_a)
            b_ref[pl.ds(r0, coef_rows), :] = jnp.sqrt(-2.0 * th / (1.0 - th)) * (i * u)
        return carry

    lax.fori_loop(0, total // coef_rows, coef_body, 0)

    n_chunks = total // SUBLANES
    ctx_chunks = ctx // SUBLANES

    def scan_body(j, carry):
        hf, hb = carry
        rf = pl.multiple_of(j * SUBLANES, SUBLANES)
        a, b = _chunk_scan(af_ref[pl.ds(rf, SUBLANES), :], bf_ref[pl.ds(rf, SUBLANES), :], False)
        h = a * hf + b
        bf_ref[pl.ds(rf, SUBLANES), :] = h
        hf = jnp.broadcast_to(h[SUBLANES - 1:SUBLANES, :], h.shape)
        jb = jnp.where(j < ctx_chunks, ctx_chunks - 1 - j, n_chunks - 1 + ctx_chunks - j)
        rb = pl.multiple_of(jb * SUBLANES, SUBLANES)
        a, b = _chunk_scan(ab_ref[pl.ds(rb, SUBLANES), :], bb_ref[pl.ds(rb, SUBLANES), :], True)
        h = a * hb + b
        bb_ref[pl.ds(rb, SUBLANES), :] = h
        hb = jnp.broadcast_to(h[0:1, :], h.shape)
        return hf, hb

    zero = jnp.zeros((SUBLANES, c), F32)
    lax.fori_loop(0, n_chunks, scan_body, (zero, zero), unroll=2)

    hsum = bf_ref[ctx:total, :] + bb_ref[ctx:total, :]
    o_ref[0] = (_gelu_tanh(gx_ref[0]) * hsum).astype(o_ref.dtype)


def _rglru(rx, rc, gx, conv_w, conv_b, gate_w, gate_b, lam):
    b, s, d_rnn = rx.shape
    ctx = rc.shape[1]
    nb = d_rnn // RG_BW
    total = ctx + s
    gw = jnp.transpose(gate_w, (2, 3, 0, 1, 4)).reshape(nb, RG_BW, 4 * RG_BW).astype(BF16)
    gb = jnp.transpose(gate_b.reshape(2, 2, nb, RG_BW), (2, 0, 1, 3)).reshape(nb, 1, 4 * RG_BW)
    seq = lambda t: pl.BlockSpec((1, t, RG_BW), lambda i, j: (i, 0, j))
    col = lambda r: pl.BlockSpec((r, RG_BW), lambda i, j: (0, j))
    scratch = [pltpu.VMEM((s + 2 * CONV_PAD, RG_BW), F32)]
    scratch += [pltpu.VMEM((total, RG_BW), F32) for _ in range(5)]
    return pl.pallas_call(
        _rglru_kernel,
        grid=(b, nb),
        in_specs=[seq(s), seq(ctx), seq(s), col(CONV_W), col(1),
                  pl.BlockSpec((1, RG_BW, 4 * RG_BW), lambda i, j: (j, 0, 0)),
                  pl.BlockSpec((1, 1, 4 * RG_BW), lambda i, j: (j, 0, 0)),
                  col(2)],
        out_specs=seq(s),
        out_shape=jax.ShapeDtypeStruct((b, s, d_rnn), BF16),
        scratch_shapes=scratch,
        compiler_params=_cparams(2, 48),
        name="rglru",
    )(rx, rc, gx, conv_w, conv_b.reshape(1, d_rnn), gw, gb, lam)


def kernel(x, c, ctx, c_ctx, ada_w, ada_b, norm_g, ffn_w_gu, ffn_w_down, ab_w_in, ab_w_out, pool_w,
           pool_scale, na_rpb, rg_w_in, rg_conv_w, rg_conv_b, rg_gate_w, rg_gate_b, rg_lambda,
           rg_w_out, final_g):
    b, s, d = x.shape
    n_ctx = ctx.shape[1]
    depth = ada_w.shape[0]
    pool_width = pool_w.shape[1] * pool_w.shape[2]
    na_width = (ab_w_in.shape[-1] - pool_width) // 3
    d_rnn = rg_w_out.shape[1]
    na_scale = NA_HEAD_DIM ** -0.5

    mods = _mods(c, c_ctx, ada_w, ada_b)
    wgu = ffn_w_gu.astype(BF16)
    wd = ffn_w_down.astype(BF16)
    cflat = ctx.reshape(1, b * n_ctx, d)

    for i in range(depth):
        last = i == depth - 1
        mx = mods[i, :b]
        mc = mods[i, b:b + 1]
        g = norm_g[i]
        j = i // 2
        fg = final_g.reshape(1, d) if last else None
        if i % 2 == 0:
            win = ab_w_in[j].astype(BF16)
            wout = ab_w_out[j].astype(BF16)
            pw = pool_w[j].astype(BF16)
            ps = pool_scale[j].reshape(1, pool_width)
            splits = ((pool_width, 1.0), (na_width, na_scale), (na_width, 1.0), (na_width, 1.0))
            dts = (F32, BF16, BF16, BF16)
            x, px, qx, kx, vx = _pre(x, mx, g, wgu[i, 0], wd[i, 0], win, splits, dts, f"pre{i}_x")
            cflat, pc, qc, kc, vc = _pre(cflat, mc, g, wgu[i, 0], wd[i, 0], win, splits, dts, f"pre{i}_c")
            per_b = lambda a: a.reshape(b, n_ctx, a.shape[-1])
            pc, qc, kc, vc = per_b(pc), per_b(qc), per_b(kc), per_b(vc)
            a_x = _pool(px, pw, ps, f"pool{i}_x")
            b_x = _nattn(qx, kx, vx, kc, vc, na_rpb[j])
            x = _post(x, mx, g, [a_x, b_x], wout, wgu[i, 1], wd[i, 1], fg, f"post{i}_x")
            if not last:
                a_c = _pool(pc, pw, ps, f"pool{i}_c")
                b_c = _ctx_attn(qc, kc, vc)
                flat = lambda a: a.reshape(1, b * n_ctx, a.shape[-1])
                cflat = _post(cflat, mc, g, [flat(a_c), flat(b_c)], wout, wgu[i, 1], wd[i, 1], None,
                              f"post{i}_c")
        else:
            win = rg_w_in[j].astype(BF16)
            wout = rg_w_out[j].astype(BF16)
            splits = ((d_rnn, 1.0), (d_rnn, 1.0))
            dts = (F32, F32)
            x, gx, rx = _pre(x, mx, g, wgu[i, 0], wd[i, 0], win, splits, dts, f"pre{i}_x")
            cflat, gc, rc = _pre(cflat, mc, g, wgu[i, 0], wd[i, 0], win, splits, dts, f"pre{i}_c")
            rc = rc.reshape(b, n_ctx, d_rnn)
            yx = _rglru(rx, rc, gx, rg_conv_w[j], rg_conv_b[j], rg_gate_w[j], rg_gate_b[j], rg_lambda[j])
            x = _post(x, mx, g, [yx], wout, wgu[i, 1], wd[i, 1], fg, f"post{i}_x")
            if not last:
                raise NotImplementedError("context output of an RG-LRU layer is not needed at this depth")
    return x
```

```python
import functools

import numpy as np
import jax
import jax.numpy as jnp
from jax import lax
from jax.experimental import pallas as pl
from jax.experimental.pallas import tpu as pltpu

F32 = jnp.float32
BF16 = jnp.bfloat16

N_MOD = 9
RMS_EPS = 1e-6
GRID_W = 64
POOL_WINDOWS = (2, 4, 8, 16)
NA_HEAD_DIM = 64
NB_ROWS = 8
NB_COLS = 16
RG_BW = 128
CONV_W = 4
RG_C = 8.0

LANES = 128
SUBLANES = 8
MXU_DIM = 256
VMEM_BYTES = 64 * 1024 * 1024

HEADS_PER_GROUP = MXU_DIM // NA_HEAD_DIM
TOKEN_TILE = 512
FF_CHUNK = MXU_DIM


def _cparams(n_axes, vmem_mb):
    return pltpu.CompilerParams(
        dimension_semantics=("arbitrary",) * n_axes,
        vmem_limit_bytes=min(vmem_mb * 1024 * 1024, VMEM_BYTES - 8 * 1024 * 1024),
    )


def _resident(shape):
    nd = len(shape)
    return pl.BlockSpec(shape, lambda *_: (0,) * nd, pipeline_mode=pl.Buffered(1))


def _silu(x):
    return x * jax.nn.sigmoid(x)


def _gelu_tanh(x):
    c = np.sqrt(2.0 / np.pi).astype(np.float32)
    return 0.5 * x * (1.0 + jnp.tanh(c * (x + 0.044715 * (x * x * x))))


def _rms_mod(x, g, shift, scale):
    ms = jnp.mean(x * x, axis=-1, keepdims=True)
    y = (x * lax.rsqrt(ms + RMS_EPS)) * g
    return y * (1.0 + scale) + shift


def _swiglu(h, wgu_ref, wd_ref, act_ref):
    d_ff = wd_ref.shape[0]
    for c in range(d_ff // FF_CHUNK):
        lo = c * FF_CHUNK
        ug = jnp.dot(h, wgu_ref[:, lo:lo + FF_CHUNK], preferred_element_type=F32)
        uu = jnp.dot(h, wgu_ref[:, d_ff + lo:d_ff + lo + FF_CHUNK], preferred_element_type=F32)
        act_ref[:, lo:lo + FF_CHUNK] = (_silu(ug) * uu).astype(BF16)
    return jnp.dot(act_ref[...], wd_ref[...], preferred_element_type=F32)


def _mods_kernel(c_ref, w_ref, b_ref, o_ref):
    h = _silu(c_ref[...]).astype(BF16)
    w = w_ref[0].astype(BF16)
    o_ref[0] = jnp.dot(h, w, preferred_element_type=F32) + b_ref[0]


def _mods(c, c_ctx, ada_w, ada_b):
    depth, d, n = ada_w.shape
    b = c.shape[0]
    rows = jnp.zeros((SUBLANES, d), F32).at[:b].set(c).at[b].set(c_ctx)
    tn = n // 4
    out = pl.pallas_call(
        _mods_kernel,
        grid=(depth, n // tn),
        in_specs=[
            pl.BlockSpec((SUBLANES, d), lambda i, j: (0, 0)),
            pl.BlockSpec((1, d, tn), lambda i, j: (i, 0, j)),
            pl.BlockSpec((1, 1, tn), lambda i, j: (i, 0, j)),
        ],
        out_specs=pl.BlockSpec((1, SUBLANES, tn), lambda i, j: (i, 0, j)),
        out_shape=jax.ShapeDtypeStruct((depth, SUBLANES, n), F32),
        compiler_params=_cparams(2, 40),
        name="mods",
    )(rows, ada_w, ada_b.reshape(depth, 1, n))
    return out.reshape(depth, SUBLANES, N_MOD, d)


def _pre_kernel(x_ref, mod_ref, g_ref, wgu_ref, wd_ref, win_ref, x_out, *rest, splits):
    outs, act_ref = rest[:-1], rest[-1]
    x = x_ref[0]
    h = _rms_mod(x, g_ref[0:1, :], mod_ref[0, 0:1, :], mod_ref[0, 1:2, :]).astype(BF16)
    y = _swiglu(h, wgu_ref, wd_ref, act_ref)
    x = x + (0.5 * mod_ref[0, 2:3, :]) * y
    x_out[0] = x
    h = _rms_mod(x, g_ref[1:2, :], mod_ref[0, 3:4, :], mod_ref[0, 4:5, :]).astype(BF16)
    off = 0
    for o_ref, (width, scale, slab) in zip(outs, splits):
        u = jnp.dot(h, win_ref[:, off:off + width], preferred_element_type=F32)
        if scale != 1.0:
            u = u * scale
        if slab:
            for j in range(width // LANES):
                o_ref[0, j] = u[:, j * LANES:(j + 1) * LANES].astype(o_ref.dtype)
        else:
            o_ref[0] = u.astype(o_ref.dtype)
        off += width


def _pre(x, mod, g, wgu, wd, win, splits, dtypes, name):
    n, t, d = x.shape
    tm = min(TOKEN_TILE, t)
    tok = lambda w: pl.BlockSpec((1, tm, w), lambda b, i: (b, i, 0))
    slab = lambda w: pl.BlockSpec((1, w // LANES, tm, LANES), lambda b, i: (b, 0, i, 0))
    out_shape = [jax.ShapeDtypeStruct((n, t, d), F32)]
    out_specs = [tok(d)]
    for (w, _, is_slab), dt in zip(splits, dtypes):
        if is_slab:
            out_shape.append(jax.ShapeDtypeStruct((n, w // LANES, t, LANES), dt))
            out_specs.append(slab(w))
        else:
            out_shape.append(jax.ShapeDtypeStruct((n, t, w), dt))
            out_specs.append(tok(w))
    return pl.pallas_call(
        functools.partial(_pre_kernel, splits=splits),
        grid=(n, t // tm),
        in_specs=[
            tok(d),
            pl.BlockSpec((1, N_MOD, d), lambda b, i: (b, 0, 0)),
            _resident(g.shape), _resident(wgu.shape), _resident(wd.shape), _resident(win.shape),
        ],
        out_specs=out_specs,
        out_shape=out_shape,
        scratch_shapes=[pltpu.VMEM((tm, wd.shape[0]), BF16)],
        compiler_params=_cparams(2, 56),
        name=name,
    )(x, mod, g, wgu, wd, win)


def _post_kernel(x_ref, mod_ref, g_ref, *rest, n_parts, final, gated_scan):
    y_refs = rest[:n_parts]
    wout_ref, wgu_ref, wd_ref = rest[n_parts:n_parts + 3]
    rest = rest[n_parts + 3:]
    if final:
        gf_ref, rest = rest[0], rest[1:]
    o_ref, act_ref = rest[:2]
    if gated_scan:
        gate_ref, hf_ref, hb_ref = y_refs
        mix_ref = rest[2]
        for j in range(hf_ref.shape[1]):
            gate = gate_ref[0, :, j * LANES:(j + 1) * LANES].astype(F32)
            hsum = hf_ref[0, j].astype(F32) + hb_ref[0, j].astype(F32)
            mix_ref[:, j * LANES:(j + 1) * LANES] = (_gelu_tanh(gate) * hsum).astype(BF16)
        acc = jnp.dot(mix_ref[...], wout_ref[...], preferred_element_type=F32)
    else:
        acc = None
        off = 0
        for y_ref in y_refs:
            w = y_ref.shape[-1]
            part = jnp.dot(y_ref[0], wout_ref[off:off + w, :], preferred_element_type=F32)
            acc = part if acc is None else acc + part
            off += w
    x = x_ref[0] + mod_ref[0, 5:6, :] * acc
    h = _rms_mod(x, g_ref[2:3, :], mod_ref[0, 6:7, :], mod_ref[0, 7:8, :]).astype(BF16)
    y = _swiglu(h, wgu_ref, wd_ref, act_ref)
    x = x + (0.5 * mod_ref[0, 8:9, :]) * y
    if final:
        ms = jnp.mean(x * x, axis=-1, keepdims=True)
        x = (x * lax.rsqrt(ms + RMS_EPS)) * gf_ref[...]
    o_ref[0] = x


def _post(x, mod, g, parts, wout, wgu, wd, final_g, name, gated_scan=False):
    n, t, d = x.shape
    tm = min(TOKEN_TILE, t)
    tok = lambda w: pl.BlockSpec((1, tm, w), lambda b, i: (b, i, 0))
    slab = lambda k: pl.BlockSpec((1, k, tm, LANES), lambda b, i: (b, 0, i, 0))
    args = [x, mod, g, *parts, wout, wgu, wd]
    in_specs = [tok(d), pl.BlockSpec((1, N_MOD, d), lambda b, i: (b, 0, 0)), _resident(g.shape)]
    in_specs += [tok(p.shape[-1]) if p.ndim == 3 else slab(p.shape[1]) for p in parts]
    in_specs += [_resident(wout.shape), _resident(wgu.shape), _resident(wd.shape)]
    if final_g is not None:
        args.append(final_g)
        in_specs.append(_resident(final_g.shape))
    scratch = [pltpu.VMEM((tm, wd.shape[0]), BF16)]
    if gated_scan:
        scratch.append(pltpu.VMEM((tm, wout.shape[0]), BF16))
    return pl.pallas_call(
        functools.partial(_post_kernel, n_parts=len(parts), final=final_g is not None,
                          gated_scan=gated_scan),
        grid=(n, t // tm),
        in_specs=in_specs,
        out_specs=tok(d),
        out_shape=jax.ShapeDtypeStruct((n, t, d), F32),
        scratch_shapes=scratch,
        compiler_params=_cparams(2, 56),
        name=name,
    )(*args)


POOL_PAD = SUBLANES * 2


def _pool_kernel(p_ref, w_ref, s_ref, o_ref, pad_ref):
    t = p_ref.shape[1]
    gw = w_ref.shape[-1]
    zeros = jnp.zeros((POOL_PAD, gw), F32)
    pad_ref[0:POOL_PAD, :] = zeros
    pad_ref[POOL_PAD + t:POOL_PAD + t + POOL_PAD, :] = zeros
    pos = lax.broadcasted_iota(jnp.int32, (t, gw), 0)
    for gi, win in enumerate(POOL_WINDOWS):
        half = win // 2
        u = p_ref[0, :, gi * gw:(gi + 1) * gw]
        pad_ref[POOL_PAD:POOL_PAD + t, :] = u
        wsum = None
        for k in range(-half, half):
            piece = pad_ref[POOL_PAD + k:POOL_PAD + k + t, :]
            wsum = piece if wsum is None else wsum + piece
        cnt = (jnp.minimum(pos + half, t) - jnp.maximum(pos - half, 0)).astype(F32)
        y = (wsum / cnt - u).astype(BF16)
        z = jnp.dot(y, w_ref[gi], preferred_element_type=F32)
        o_ref[0, :, gi * gw:(gi + 1) * gw] = (z * s_ref[:, gi * gw:(gi + 1) * gw]).astype(o_ref.dtype)


def _pool(p, pool_w, pool_scale, name):
    n, t, w = p.shape
    gw = pool_w.shape[-1]
    return pl.pallas_call(
        _pool_kernel,
        grid=(n,),
        in_specs=[
            pl.BlockSpec((1, t, w), lambda b: (b, 0, 0)),
            _resident(pool_w.shape), _resident(pool_scale.shape),
        ],
        out_specs=pl.BlockSpec((1, t, w), lambda b: (b, 0, 0)),
        out_shape=jax.ShapeDtypeStruct((n, t, w), BF16),
        scratch_shapes=[pltpu.VMEM((t + 2 * POOL_PAD, gw), F32)],
        compiler_params=_cparams(1, 48),
        name=name,
    )(p, pool_w, pool_scale)


def _head_block_mask(shape):
    r = lax.broadcasted_iota(jnp.int32, shape, 0) // NA_HEAD_DIM
    c = lax.broadcasted_iota(jnp.int32, shape, 1) // NA_HEAD_DIM
    return r == c


def _attend(q_blk, key_parts, val_parts, bias_parts):
    nq, g = q_blk.shape
    heads = g // NA_HEAD_DIM
    qbd = jnp.where(_head_block_mask((heads * nq, g)),
                    jnp.concatenate([q_blk] * heads, axis=0), jnp.zeros((), BF16))
    nt = (((1,), (1,)), ((), ()))
    scores = []
    for keys, bias in zip(key_parts, bias_parts):
        s = lax.dot_general(keys, qbd, nt, preferred_element_type=F32)
        scores.append(s if bias is None else s + bias)
    m = functools.reduce(jnp.maximum, [jnp.max(s, axis=0, keepdims=True) for s in scores])
    ps = [jnp.exp(s - m) for s in scores]
    l = functools.reduce(jnp.add, [jnp.sum(p, axis=0, keepdims=True) for p in ps])
    inv = 1.0 / l
    tn = (((0,), (0,)), ((), ()))
    r = None
    for p, vals in zip(ps, val_parts):
        part = lax.dot_general((p * inv).astype(BF16), vals, tn, preferred_element_type=F32)
        r = part if r is None else r + part
    lane_head = lax.broadcasted_iota(jnp.int32, (nq, g), 1) // NA_HEAD_DIM
    out = jnp.zeros((nq, g), F32)
    for h in range(heads):
        out = jnp.where(lane_head == h, r[h * nq:(h + 1) * nq, :], out)
    return out


def _nattn_kernel(q_ref, k_ref, v_ref, kc_ref, vc_ref, bias_ref, o_ref):
    s = q_ref.shape[1]
    rows = s // GRID_W
    kh = min(NB_ROWS, rows)
    kc = kc_ref[0]
    vc = vc_ref[0]

    def body(i, carry):
        r0 = jnp.clip(i - kh // 2, 0, rows - kh)
        q0 = pl.multiple_of(i * GRID_W, GRID_W)
        k0 = pl.multiple_of(r0 * GRID_W, GRID_W)
        q_blk = q_ref[0, pl.ds(q0, GRID_W), :]
        keys = k_ref[0, pl.ds(k0, kh * GRID_W), :]
        vals = v_ref[0, pl.ds(k0, kh * GRID_W), :]
        bias = bias_ref[0, pl.ds(r0 - i + (NB_ROWS - 1), kh), :, :]
        bias = bias.reshape(kh * GRID_W, bias.shape[-1])
        out = _attend(q_blk, [keys, kc], [vals, vc], [bias, None])
        o_ref[0, pl.ds(q0, GRID_W), :] = out.astype(o_ref.dtype)
        return carry

    lax.fori_loop(0, rows, body, 0)


def _ctx_attn_kernel(q_ref, kc_ref, vc_ref, o_ref):
    t = q_ref.shape[1]
    kc = kc_ref[0]
    vc = vc_ref[0]
    for i in range(t // GRID_W):
        q_blk = q_ref[0, i * GRID_W:(i + 1) * GRID_W, :]
        out = _attend(q_blk, [kc], [vc], [None])
        o_ref[0, i * GRID_W:(i + 1) * GRID_W, :] = out.astype(o_ref.dtype)


def _bias_table(rpb):
    h = rpb.shape[0]
    kcol = np.arange(GRID_W)[:, None]
    qcol = np.arange(GRID_W)[None, :]
    qstart = np.clip(qcol - NB_COLS // 2, 0, GRID_W - NB_COLS)
    mask = (kcol >= qstart) & (kcol < qstart + NB_COLS)
    idx = np.clip(kcol - qcol + NB_COLS - 1, 0, 2 * NB_COLS - 2)
    t = jnp.where(mask[None, None], rpb.astype(F32)[:, :, idx], -jnp.inf)
    t = t.reshape(h // HEADS_PER_GROUP, HEADS_PER_GROUP, t.shape[1], GRID_W, GRID_W)
    t = jnp.transpose(t, (0, 2, 3, 1, 4))
    return t.reshape(h // HEADS_PER_GROUP, t.shape[1], GRID_W, HEADS_PER_GROUP * GRID_W)


def _nattn(q, k, v, kc, vc, rpb):
    b, s, w = q.shape
    ctx = kc.shape[1]
    g = HEADS_PER_GROUP * NA_HEAD_DIM
    bias = _bias_table(rpb)
    seq = lambda t: pl.BlockSpec((1, t, g), lambda i, j: (i, 0, j))
    return pl.pallas_call(
        _nattn_kernel,
        grid=(b, w // g),
        in_specs=[seq(s), seq(s), seq(s), seq(ctx), seq(ctx),
                  pl.BlockSpec((1,) + bias.shape[1:], lambda i, j: (j, 0, 0, 0))],
        out_specs=seq(s),
        out_shape=jax.ShapeDtypeStruct((b, s, w), BF16),
        compiler_params=_cparams(2, 48),
        name="nattn",
    )(q, k, v, kc, vc, bias)


def _ctx_attn(q, kc, vc):
    b, t, w = q.shape
    g = HEADS_PER_GROUP * NA_HEAD_DIM
    seq = pl.BlockSpec((1, t, g), lambda i, j: (i, 0, j))
    return pl.pallas_call(
        _ctx_attn_kernel,
        grid=(b, w // g),
        in_specs=[seq, seq, seq],
        out_specs=seq,
        out_shape=jax.ShapeDtypeStruct((b, t, w), BF16),
        compiler_params=_cparams(2, 32),
        name="ctx_attn",
    )(q, kc, vc)


RG_CHUNK = 128
HALO = SUBLANES
RG_PITCH = RG_CHUNK + SUBLANES
SCAN_UNROLL = 8


def _rglru_kernel(rxf_ref, rxf_prev_ref, rxf_next_ref, rxb_ref, rxb_prev_ref, rxb_next_ref, rc_ref,
                  cw_ref, cb_ref, gw_ref, gb_ref, lam_ref, hf_ref, hb_ref,
                  winf_ref, winb_ref, af_ref, bf_ref, ab_ref, bb_ref, state_ref, *, n_ctx):
    nbatch, nblk, tc, _ = rxf_ref.shape
    ctx_chunks = n_ctx // tc
    x_chunks = pl.num_programs(0) - ctx_chunks
    c = pl.program_id(0)
    in_ctx = c < ctx_chunks

    @pl.when(c == 0)
    def _():
        state_ref[...] = jnp.zeros(state_ref.shape, F32)

    def fill_from_ctx(win_ref, j):
        def body(n, carry):
            for b in range(nbatch):
                base = b * n_ctx + j * tc
                prev0 = pl.multiple_of(jnp.maximum(base - HALO, b * n_ctx), HALO)
                next0 = pl.multiple_of(jnp.minimum(base + tc, (b + 1) * n_ctx - HALO), HALO)
                win_ref[n, b, 0:HALO, :] = jnp.where(j > 0, rc_ref[n, pl.ds(prev0, HALO), :], 0.0)
                win_ref[n, b, HALO:HALO + tc, :] = rc_ref[n, pl.ds(pl.multiple_of(base, tc), tc), :]
                win_ref[n, b, HALO + tc:2 * HALO + tc, :] = jnp.where(
                    j < ctx_chunks - 1, rc_ref[n, pl.ds(next0, HALO), :], 0.0)
            return carry
        lax.fori_loop(0, nblk, body, 0)

    def fill_from_x(win_ref, cur_ref, prev_ref, next_ref, j):
        def body(n, carry):
            for b in range(nbatch):
                win_ref[n, b, 0:HALO, :] = jnp.where(j > 0, prev_ref[b, n], 0.0)
                win_ref[n, b, HALO:HALO + tc, :] = cur_ref[b, n]
                win_ref[n, b, HALO + tc:2 * HALO + tc, :] = jnp.where(j < x_chunks - 1, next_ref[b, n], 0.0)
            return carry
        lax.fori_loop(0, nblk, body, 0)

    @pl.when(in_ctx)
    def _():
        fill_from_ctx(winf_ref, c)
        fill_from_ctx(winb_ref, ctx_chunks - 1 - c)

    @pl.when(jnp.logical_not(in_ctx))
    def _():
        xf = c - ctx_chunks
        fill_from_x(winf_ref, rxf_ref, rxf_prev_ref, rxf_next_ref, xf)
        fill_from_x(winb_ref, rxb_ref, rxb_prev_ref, rxb_next_ref, x_chunks - 1 - xf)

    def coefficients(win_ref, d, a_ref, b_ref):
        cols = slice(d * 2 * RG_BW, (d + 1) * 2 * RG_BW)

        def body(n, carry):
            cw = cw_ref[n]
            us = []
            for b in range(nbatch):
                y = cb_ref[n]
                for k in range(CONV_W):
                    lo = HALO + k - CONV_W // 2
                    y = y + win_ref[n, b, lo:lo + tc, :] * cw[k:k + 1, :]
                us.append(y)
            u = jnp.concatenate(us, axis=0)
            z = jnp.dot(u.astype(BF16), gw_ref[n, :, cols], preferred_element_type=F32) + gb_ref[n, :, cols]
            t_r = jnp.tanh(z[:, :RG_BW])
            t_i = jnp.tanh(z[:, RG_BW:])
            lam = -lam_ref[n, d:d + 1, :]
            softplus = jnp.maximum(lam, 0.0) + jnp.log1p(jnp.exp(-jnp.abs(lam)))
            log_a = (t_r + 1.0) * ((-0.5 * RG_C) * softplus)
            a = jnp.exp(log_a)
            th = jnp.tanh(log_a)
            num = -2.0 * th
            root = jnp.where(num > 0.0, num * lax.rsqrt(num * (1.0 - th)), 0.0)
            coef = root * ((t_i + 1.0) * (0.5 * u))
            for b in range(nbatch):
                a_ref[n, b * RG_PITCH:b * RG_PITCH + tc, :] = a[b * tc:(b + 1) * tc]
                b_ref[n, b * RG_PITCH:b * RG_PITCH + tc, :] = coef[b * tc:(b + 1) * tc]
            return carry

        lax.fori_loop(0, nblk, body, 0)

    coefficients(winf_ref, 0, af_ref, bf_ref)
    coefficients(winb_ref, 1, ab_ref, bb_ref)

    def scan_step(t, hs):
        out_f, out_b = [], []
        rows_f = pl.ds(t, nbatch, stride=RG_PITCH)
        rows_b = pl.ds(tc - 1 - t, nbatch, stride=RG_PITCH)
        for n in range(nblk):
            h = af_ref[n, rows_f, :] * hs[n] + bf_ref[n, rows_f, :]
            bf_ref[n, rows_f, :] = h
            out_f.append(h)
            h = ab_ref[n, rows_b, :] * hs[nblk + n] + bb_ref[n, rows_b, :]
            bb_ref[n, rows_b, :] = h
            out_b.append(h)
        return tuple(out_f + out_b)

    hs = tuple(state_ref[d, n] for d in range(2) for n in range(nblk))
    hs = lax.fori_loop(0, tc, scan_step, hs, unroll=SCAN_UNROLL)
    for d in range(2):
        for n in range(nblk):
            state_ref[d, n] = hs[d * nblk + n]

    @pl.when(jnp.logical_not(in_ctx))
    def _():
        def emit(n, carry):
            for b in range(nbatch):
                rows = slice(b * RG_PITCH, b * RG_PITCH + tc)
                hf_ref[b, n] = bf_ref[n, rows, :].astype(hf_ref.dtype)
                hb_ref[b, n] = bb_ref[n, rows, :].astype(hb_ref.dtype)
            return carry
        lax.fori_loop(0, nblk, emit, 0)


def _rglru(rx, rc, n_ctx, conv_w, conv_b, gate_w, gate_b, lam):
    b, nb, s, _ = rx.shape
    tc = RG_CHUNK
    ctx_chunks = n_ctx // tc
    x_chunks = s // tc
    per_halo = tc // HALO
    gw = (0.5 * jnp.transpose(gate_w, (2, 3, 0, 1, 4))).reshape(nb, RG_BW, 4 * RG_BW).astype(BF16)
    gb = 0.5 * jnp.transpose(gate_b.reshape(2, 2, nb, RG_BW), (2, 0, 1, 3)).reshape(nb, 1, 4 * RG_BW)
    cw = jnp.transpose(conv_w.reshape(CONV_W, nb, RG_BW), (1, 0, 2))
    cb = conv_b.reshape(nb, 1, RG_BW)
    lam_s = jnp.transpose(lam.reshape(2, nb, RG_BW), (1, 0, 2))

    chunk_f = lambda c: jnp.maximum(c - ctx_chunks, 0)
    chunk_b = lambda c: jnp.minimum(x_chunks - 1 + ctx_chunks - c, x_chunks - 1)
    cur = lambda f: pl.BlockSpec((b, nb, tc, LANES), lambda c: (0, 0, f(c), 0))
    prev = lambda f: pl.BlockSpec((b, nb, HALO, LANES),
                                  lambda c: (0, 0, jnp.maximum(f(c) * per_halo - 1, 0), 0))
    nxt = lambda f: pl.BlockSpec((b, nb, HALO, LANES),
                                 lambda c: (0, 0, jnp.minimum((f(c) + 1) * per_halo, s // HALO - 1), 0))
    slab = pltpu.VMEM((nb, b * RG_PITCH, LANES), F32)
    window = pltpu.VMEM((nb, b, tc + 2 * HALO, LANES), F32)
    state_shape = jax.ShapeDtypeStruct((b, nb, s, LANES), BF16)
    return pl.pallas_call(
        functools.partial(_rglru_kernel, n_ctx=n_ctx),
        grid=(ctx_chunks + x_chunks,),
        in_specs=[cur(chunk_f), prev(chunk_f), nxt(chunk_f), cur(chunk_b), prev(chunk_b), nxt(chunk_b),
                  _resident(rc.shape), _resident(cw.shape), _resident(cb.shape), _resident(gw.shape),
                  _resident(gb.shape), _resident(lam_s.shape)],
        out_specs=[cur(chunk_f), cur(chunk_b)],
        out_shape=[state_shape, state_shape],
        scratch_shapes=[window, window, slab, slab, slab, slab, pltpu.VMEM((2, nb, b, LANES), F32)],
        compiler_params=_cparams(1, 52),
        name="rglru",
    )(rx, rx, rx, rx, rx, rx, rc, cw, cb, gw, gb, lam_s)


def kernel(x, c, ctx, c_ctx, ada_w, ada_b, norm_g, ffn_w_gu, ffn_w_down, ab_w_in, ab_w_out, pool_w,
           pool_scale, na_rpb, rg_w_in, rg_conv_w, rg_conv_b, rg_gate_w, rg_gate_b, rg_lambda,
           rg_w_out, final_g):
    b, s, d = x.shape
    n_ctx = ctx.shape[1]
    depth = ada_w.shape[0]
    pool_width = pool_w.shape[1] * pool_w.shape[2]
    na_width = (ab_w_in.shape[-1] - pool_width) // 3
    d_rnn = rg_w_out.shape[1]
    na_scale = NA_HEAD_DIM ** -0.5

    mods = _mods(c, c_ctx, ada_w, ada_b)
    wgu = [[ffn_w_gu[i, k].astype(BF16) for k in range(2)] for i in range(depth)]
    wd = [[ffn_w_down[i, k].astype(BF16) for k in range(2)] for i in range(depth)]
    cflat = ctx.reshape(1, b * n_ctx, d)

    for i in range(depth):
        last = i == depth - 1
        mx = mods[i, :b]
        mc = mods[i, b:b + 1]
        g = norm_g[i]
        j = i // 2
        fg = final_g.reshape(1, d) if last else None
        if i % 2 == 0:
            win = ab_w_in[j].astype(BF16)
            wout = ab_w_out[j].astype(BF16)
            pw = pool_w[j].astype(BF16)
            ps = pool_scale[j].reshape(1, pool_width)
            splits = ((pool_width, 1.0, False), (na_width, na_scale, False), (na_width, 1.0, False),
                      (na_width, 1.0, False))
            dts = (F32, BF16, BF16, BF16)
            x, px, qx, kx, vx = _pre(x, mx, g, wgu[i][0], wd[i][0], win, splits, dts, f"pre{i}_x")
            cflat, pc, qc, kc, vc = _pre(cflat, mc, g, wgu[i][0], wd[i][0], win, splits, dts, f"pre{i}_c")
            per_b = lambda a: a.reshape(b, n_ctx, a.shape[-1])
            pc, qc, kc, vc = per_b(pc), per_b(qc), per_b(kc), per_b(vc)
            a_x = _pool(px, pw, ps, f"pool{i}_x")
            b_x = _nattn(qx, kx, vx, kc, vc, na_rpb[j])
            x = _post(x, mx, g, [a_x, b_x], wout, wgu[i][1], wd[i][1], fg, f"post{i}_x")
            if not last:
                a_c = _pool(pc, pw, ps, f"pool{i}_c")
                b_c = _ctx_attn(qc, kc, vc)
                flat = lambda a: a.reshape(1, b * n_ctx, a.shape[-1])
                cflat = _post(cflat, mc, g, [flat(a_c), flat(b_c)], wout, wgu[i][1], wd[i][1], None,
                              f"post{i}_c")
        else:
            win = rg_w_in[j].astype(BF16)
            wout = rg_w_out[j].astype(BF16)
            splits = ((d_rnn, 1.0, False), (d_rnn, 1.0, True))
            dts = (BF16, F32)
            x, gx, rx = _pre(x, mx, g, wgu[i][0], wd[i][0], win, splits, dts, f"pre{i}_x")
            cflat, gc, rc = _pre(cflat, mc, g, wgu[i][0], wd[i][0], win, splits, dts, f"pre{i}_c")
            hf, hb = _rglru(rx, rc[0], n_ctx, rg_conv_w[j], rg_conv_b[j], rg_gate_w[j], rg_gate_b[j],
                            rg_lambda[j])
            x = _post(x, mx, g, [gx, hf, hb], wout, wgu[i][1], wd[i][1], fg, f"post{i}_x",
                      gated_scan=True)
            if not last:
                raise NotImplementedError("context output of an RG-LRU layer is not needed at this depth")
    return x
```

```python
import functools

import numpy as np
import jax
import jax.numpy as jnp
from jax import lax
from jax.experimental import pallas as pl
from jax.experimental.pallas import tpu as pltpu

F32 = jnp.float32
BF16 = jnp.bfloat16

N_MOD = 9
RMS_EPS = 1e-6
GRID_W = 64
POOL_WINDOWS = (2, 4, 8, 16)
NA_HEAD_DIM = 64
NB_ROWS = 8
NB_COLS = 16
RG_BW = 128
CONV_W = 4
RG_C = 8.0

LANES = 128
SUBLANES = 8
MXU_DIM = 256
VMEM_BYTES = 64 * 1024 * 1024

HEADS_PER_GROUP = MXU_DIM // NA_HEAD_DIM
TOKEN_TILE = 512
FF_CHUNK = MXU_DIM


def _cparams(n_axes, vmem_mb):
    return pltpu.CompilerParams(
        dimension_semantics=("arbitrary",) * n_axes,
        vmem_limit_bytes=min(vmem_mb * 1024 * 1024, VMEM_BYTES - 8 * 1024 * 1024),
    )


def _resident(shape):
    nd = len(shape)
    return pl.BlockSpec(shape, lambda *_: (0,) * nd, pipeline_mode=pl.Buffered(1))


def _silu(x):
    return x * jax.nn.sigmoid(x)


def _gelu_tanh(x):
    c = np.sqrt(2.0 / np.pi).astype(np.float32)
    return 0.5 * x * (1.0 + jnp.tanh(c * (x + 0.044715 * (x * x * x))))


def _rms_mod(x, g, shift, scale):
    ms = jnp.mean(x * x, axis=-1, keepdims=True)
    y = (x * lax.rsqrt(ms + RMS_EPS)) * g
    return y * (1.0 + scale) + shift


def _swiglu(h, wgu_ref, wd_ref, act_ref):
    d_ff = wd_ref.shape[0]
    for c in range(d_ff // FF_CHUNK):
        lo = c * FF_CHUNK
        ug = jnp.dot(h, wgu_ref[:, lo:lo + FF_CHUNK], preferred_element_type=F32)
        uu = jnp.dot(h, wgu_ref[:, d_ff + lo:d_ff + lo + FF_CHUNK], preferred_element_type=F32)
        act_ref[:, lo:lo + FF_CHUNK] = (_silu(ug) * uu).astype(BF16)
    return jnp.dot(act_ref[...], wd_ref[...], preferred_element_type=F32)


def _mods_kernel(c_ref, w_ref, b_ref, o_ref):
    h = _silu(c_ref[...]).astype(BF16)
    w = w_ref[0].astype(BF16)
    o_ref[0] = jnp.dot(h, w, preferred_element_type=F32) + b_ref[0]


def _mods(c, c_ctx, ada_w, ada_b):
    depth, d, n = ada_w.shape
    b = c.shape[0]
    rows = jnp.zeros((SUBLANES, d), F32).at[:b].set(c).at[b].set(c_ctx)
    tn = n // 4
    out = pl.pallas_call(
        _mods_kernel,
        grid=(depth, n // tn),
        in_specs=[
            pl.BlockSpec((SUBLANES, d), lambda i, j: (0, 0)),
            pl.BlockSpec((1, d, tn), lambda i, j: (i, 0, j)),
            pl.BlockSpec((1, 1, tn), lambda i, j: (i, 0, j)),
        ],
        out_specs=pl.BlockSpec((1, SUBLANES, tn), lambda i, j: (i, 0, j)),
        out_shape=jax.ShapeDtypeStruct((depth, SUBLANES, n), F32),
        compiler_params=_cparams(2, 40),
        name="mods",
    )(rows, ada_w, ada_b.reshape(depth, 1, n))
    return out.reshape(depth, SUBLANES, N_MOD, d)


def _pre_kernel(x_ref, mod_ref, g_ref, wgu_ref, wd_ref, win_ref, x_out, *rest, splits):
    outs, act_ref = rest[:-1], rest[-1]
    x = x_ref[0]
    h = _rms_mod(x, g_ref[0:1, :], mod_ref[0, 0:1, :], mod_ref[0, 1:2, :]).astype(BF16)
    y = _swiglu(h, wgu_ref, wd_ref, act_ref)
    x = x + (0.5 * mod_ref[0, 2:3, :]) * y
    x_out[0] = x
    h = _rms_mod(x, g_ref[1:2, :], mod_ref[0, 3:4, :], mod_ref[0, 4:5, :]).astype(BF16)
    off = 0
    for o_ref, (width, scale, slab) in zip(outs, splits):
        u = jnp.dot(h, win_ref[:, off:off + width], preferred_element_type=F32)
        if scale != 1.0:
            u = u * scale
        if slab:
            for j in range(width // LANES):
                o_ref[0, j] = u[:, j * LANES:(j + 1) * LANES].astype(o_ref.dtype)
        else:
            o_ref[0] = u.astype(o_ref.dtype)
        off += width


def _pre(x, mod, g, wgu, wd, win, splits, dtypes, name):
    n, t, d = x.shape
    tm = min(TOKEN_TILE, t)
    tok = lambda w: pl.BlockSpec((1, tm, w), lambda b, i: (b, i, 0))
    slab = lambda w: pl.BlockSpec((1, w // LANES, tm, LANES), lambda b, i: (b, 0, i, 0))
    out_shape = [jax.ShapeDtypeStruct((n, t, d), F32)]
    out_specs = [tok(d)]
    for (w, _, is_slab), dt in zip(splits, dtypes):
        if is_slab:
            out_shape.append(jax.ShapeDtypeStruct((n, w // LANES, t, LANES), dt))
            out_specs.append(slab(w))
        else:
            out_shape.append(jax.ShapeDtypeStruct((n, t, w), dt))
            out_specs.append(tok(w))
    return pl.pallas_call(
        functools.partial(_pre_kernel, splits=splits),
        grid=(n, t // tm),
        in_specs=[
            tok(d),
            pl.BlockSpec((1, N_MOD, d), lambda b, i: (b, 0, 0)),
            _resident(g.shape), _resident(wgu.shape), _resident(wd.shape), _resident(win.shape),
        ],
        out_specs=out_specs,
        out_shape=out_shape,
        scratch_shapes=[pltpu.VMEM((tm, wd.shape[0]), BF16)],
        compiler_params=_cparams(2, 56),
        name=name,
    )(x, mod, g, wgu, wd, win)


def _post_kernel(x_ref, mod_ref, g_ref, *rest, n_parts, final, gated_scan):
    y_refs = rest[:n_parts]
    wout_ref, wgu_ref, wd_ref = rest[n_parts:n_parts + 3]
    rest = rest[n_parts + 3:]
    if final:
        gf_ref, rest = rest[0], rest[1:]
    o_ref, act_ref = rest[:2]
    if gated_scan:
        gate_ref, hf_ref, hb_ref = y_refs
        mix_ref = rest[2]
        for j in range(hf_ref.shape[1]):
            gate = gate_ref[0, :, j * LANES:(j + 1) * LANES].astype(F32)
            hsum = hf_ref[0, j].astype(F32) + hb_ref[0, j].astype(F32)
            mix_ref[:, j * LANES:(j + 1) * LANES] = (_gelu_tanh(gate) * hsum).astype(BF16)
        acc = jnp.dot(mix_ref[...], wout_ref[...], preferred_element_type=F32)
    else:
        acc = None
        off = 0
        for y_ref in y_refs:
            w = y_ref.shape[-1]
            part = jnp.dot(y_ref[0], wout_ref[off:off + w, :], preferred_element_type=F32)
            acc = part if acc is None else acc + part
            off += w
    x = x_ref[0] + mod_ref[0, 5:6, :] * acc
    h = _rms_mod(x, g_ref[2:3, :], mod_ref[0, 6:7, :], mod_ref[0, 7:8, :]).astype(BF16)
    y = _swiglu(h, wgu_ref, wd_ref, act_ref)
    x = x + (0.5 * mod_ref[0, 8:9, :]) * y
    if final:
        ms = jnp.mean(x * x, axis=-1, keepdims=True)
        x = (x * lax.rsqrt(ms + RMS_EPS)) * gf_ref[...]
    o_ref[0] = x


def _post(x, mod, g, parts, wout, wgu, wd, final_g, name, gated_scan=False):
    n, t, d = x.shape
    tm = min(TOKEN_TILE, t)
    tok = lambda w: pl.BlockSpec((1, tm, w), lambda b, i: (b, i, 0))
    slab = lambda k: pl.BlockSpec((1, k, tm, LANES), lambda b, i: (b, 0, i, 0))
    args = [x, mod, g, *parts, wout, wgu, wd]
    in_specs = [tok(d), pl.BlockSpec((1, N_MOD, d), lambda b, i: (b, 0, 0)), _resident(g.shape)]
    in_specs += [tok(p.shape[-1]) if p.ndim == 3 else slab(p.shape[1]) for p in parts]
    in_specs += [_resident(wout.shape), _resident(wgu.shape), _resident(wd.shape)]
    if final_g is not None:
        args.append(final_g)
        in_specs.append(_resident(final_g.shape))
    scratch = [pltpu.VMEM((tm, wd.shape[0]), BF16)]
    if gated_scan:
        scratch.append(pltpu.VMEM((tm, wout.shape[0]), BF16))
    return pl.pallas_call(
        functools.partial(_post_kernel, n_parts=len(parts), final=final_g is not None,
                          gated_scan=gated_scan),
        grid=(n, t // tm),
        in_specs=in_specs,
        out_specs=tok(d),
        out_shape=jax.ShapeDtypeStruct((n, t, d), F32),
        scratch_shapes=scratch,
        compiler_params=_cparams(2, 56),
        name=name,
    )(*args)


POOL_PAD = SUBLANES * 2


def _pool_kernel(p_ref, w_ref, s_ref, o_ref, pad_ref):
    t = p_ref.shape[1]
    gw = w_ref.shape[-1]
    zeros = jnp.zeros((POOL_PAD, gw), F32)
    pad_ref[0:POOL_PAD, :] = zeros
    pad_ref[POOL_PAD + t:POOL_PAD + t + POOL_PAD, :] = zeros
    pos = lax.broadcasted_iota(jnp.int32, (t, gw), 0)
    for gi, win in enumerate(POOL_WINDOWS):
        half = win // 2
        u = p_ref[0, :, gi * gw:(gi + 1) * gw]
        pad_ref[POOL_PAD:POOL_PAD + t, :] = u
        wsum = None
        for k in range(-half, half):
            piece = pad_ref[POOL_PAD + k:POOL_PAD + k + t, :]
            wsum = piece if wsum is None else wsum + piece
        cnt = (jnp.minimum(pos + half, t) - jnp.maximum(pos - half, 0)).astype(F32)
        y = (wsum / cnt - u).astype(BF16)
        z = jnp.dot(y, w_ref[gi], preferred_element_type=F32)
        o_ref[0, :, gi * gw:(gi + 1) * gw] = (z * s_ref[:, gi * gw:(gi + 1) * gw]).astype(o_ref.dtype)


def _pool(p, pool_w, pool_scale, name):
    n, t, w = p.shape
    gw = pool_w.shape[-1]
    return pl.pallas_call(
        _pool_kernel,
        grid=(n,),
        in_specs=[
            pl.BlockSpec((1, t, w), lambda b: (b, 0, 0)),
            _resident(pool_w.shape), _resident(pool_scale.shape),
        ],
        out_specs=pl.BlockSpec((1, t, w), lambda b: (b, 0, 0)),
        out_shape=jax.ShapeDtypeStruct((n, t, w), BF16),
        scratch_shapes=[pltpu.VMEM((t + 2 * POOL_PAD, gw), F32)],
        compiler_params=_cparams(1, 48),
        name=name,
    )(p, pool_w, pool_scale)


def _head_block_mask(shape):
    r = lax.broadcasted_iota(jnp.int32, shape, 0) // NA_HEAD_DIM
    c = lax.broadcasted_iota(jnp.int32, shape, 1) // NA_HEAD_DIM
    return r == c


def _attend(q_blk, key_parts, val_parts, bias_parts):
    nq, g = q_blk.shape
    heads = g // NA_HEAD_DIM
    qbd = jnp.where(_head_block_mask((heads * nq, g)),
                    jnp.concatenate([q_blk] * heads, axis=0), jnp.zeros((), BF16))
    nt = (((1,), (1,)), ((), ()))
    scores = []
    for keys, bias in zip(key_parts, bias_parts):
        s = lax.dot_general(keys, qbd, nt, preferred_element_type=F32)
        scores.append(s if bias is None else s + bias)
    m = functools.reduce(jnp.maximum, [jnp.max(s, axis=0, keepdims=True) for s in scores])
    ps = [jnp.exp(s - m) for s in scores]
    l = functools.reduce(jnp.add, [jnp.sum(p, axis=0, keepdims=True) for p in ps])
    inv = 1.0 / l
    tn = (((0,), (0,)), ((), ()))
    r = None
    for p, vals in zip(ps, val_parts):
        part = lax.dot_general((p * inv).astype(BF16), vals, tn, preferred_element_type=F32)
        r = part if r is None else r + part
    lane_head = lax.broadcasted_iota(jnp.int32, (nq, g), 1) // NA_HEAD_DIM
    out = jnp.zeros((nq, g), F32)
    for h in range(heads):
        out = jnp.where(lane_head == h, r[h * nq:(h + 1) * nq, :], out)
    return out


NATTN_UNROLL = 8


def _nattn_kernel(q_ref, k_ref, v_ref, kc_ref, vc_ref, bias_ref, o_ref):
    s = q_ref.shape[1]
    rows = s // GRID_W
    kh = min(NB_ROWS, rows)
    kc = kc_ref[0]
    vc = vc_ref[0]

    def body(i, carry):
        r0 = jnp.clip(i - kh // 2, 0, rows - kh)
        q0 = pl.multiple_of(i * GRID_W, GRID_W)
        k0 = pl.multiple_of(r0 * GRID_W, GRID_W)
        q_blk = q_ref[0, pl.ds(q0, GRID_W), :]
        keys = k_ref[0, pl.ds(k0, kh * GRID_W), :]
        vals = v_ref[0, pl.ds(k0, kh * GRID_W), :]
        bias = bias_ref[0, pl.ds(r0 - i + (NB_ROWS - 1), kh), :, :]
        bias = bias.reshape(kh * GRID_W, bias.shape[-1])
        out = _attend(q_blk, [keys, kc], [vals, vc], [bias, None])
        o_ref[0, pl.ds(q0, GRID_W), :] = out.astype(o_ref.dtype)
        return carry

    lax.fori_loop(0, rows, body, 0, unroll=NATTN_UNROLL)


def _ctx_attn_kernel(q_ref, kc_ref, vc_ref, o_ref):
    t = q_ref.shape[1]
    kc = kc_ref[0]
    vc = vc_ref[0]
    for i in range(t // GRID_W):
        q_blk = q_ref[0, i * GRID_W:(i + 1) * GRID_W, :]
        out = _attend(q_blk, [kc], [vc], [None])
        o_ref[0, i * GRID_W:(i + 1) * GRID_W, :] = out.astype(o_ref.dtype)


def _bias_table(rpb):
    h = rpb.shape[0]
    kcol = np.arange(GRID_W)[:, None]
    qcol = np.arange(GRID_W)[None, :]
    qstart = np.clip(qcol - NB_COLS // 2, 0, GRID_W - NB_COLS)
    mask = (kcol >= qstart) & (kcol < qstart + NB_COLS)
    idx = np.clip(kcol - qcol + NB_COLS - 1, 0, 2 * NB_COLS - 2)
    t = jnp.where(mask[None, None], rpb.astype(F32)[:, :, idx], -jnp.inf)
    t = t.reshape(h // HEADS_PER_GROUP, HEADS_PER_GROUP, t.shape[1], GRID_W, GRID_W)
    t = jnp.transpose(t, (0, 2, 3, 1, 4))
    return t.reshape(h // HEADS_PER_GROUP, t.shape[1], GRID_W, HEADS_PER_GROUP * GRID_W)


def _nattn(q, k, v, kc, vc, rpb):
    b, s, w = q.shape
    ctx = kc.shape[1]
    g = HEADS_PER_GROUP * NA_HEAD_DIM
    bias = _bias_table(rpb)
    seq = lambda t: pl.BlockSpec((1, t, g), lambda i, j: (i, 0, j))
    return pl.pallas_call(
        _nattn_kernel,
        grid=(b, w // g),
        in_specs=[seq(s), seq(s), seq(s), seq(ctx), seq(ctx),
                  pl.BlockSpec((1,) + bias.shape[1:], lambda i, j: (j, 0, 0, 0))],
        out_specs=seq(s),
        out_shape=jax.ShapeDtypeStruct((b, s, w), BF16),
        compiler_params=_cparams(2, 48),
        name="nattn",
    )(q, k, v, kc, vc, bias)


def _ctx_attn(q, kc, vc):
    b, t, w = q.shape
    g = HEADS_PER_GROUP * NA_HEAD_DIM
    seq = pl.BlockSpec((1, t, g), lambda i, j: (i, 0, j))
    return pl.pallas_call(
        _ctx_attn_kernel,
        grid=(b, w // g),
        in_specs=[seq, seq, seq],
        out_specs=seq,
        out_shape=jax.ShapeDtypeStruct((b, t, w), BF16),
        compiler_params=_cparams(2, 32),
        name="ctx_attn",
    )(q, kc, vc)


RG_CHUNK = 128
HALO = SUBLANES
SCAN_UNROLL = 8
COEF_UNROLL = 2


def _rglru_kernel(rxf_ref, rxf_prev_ref, rxf_next_ref, rxb_ref, rxb_prev_ref, rxb_next_ref, rc_ref,
                  cw_ref, cb_ref, gw_ref, gb_ref, lam_ref, hf_ref, hb_ref,
                  win_ref, af_ref, bf_ref, ab_ref, bb_ref, sf_ref, sb_ref, state_ref, *, n_ctx):
    nbatch, nblk, tc, _ = rxf_ref.shape
    ngroups = nblk // COEF_UNROLL
    ctx_chunks = n_ctx // tc
    x_chunks = pl.num_programs(0) - ctx_chunks
    c = pl.program_id(0)
    in_ctx = c < ctx_chunks

    @pl.when(c == 0)
    def _():
        state_ref[...] = jnp.zeros(state_ref.shape, F32)

    def fill_from_ctx(win, n, j):
        for b in range(nbatch):
            base = b * n_ctx + j * tc
            prev0 = pl.multiple_of(jnp.maximum(base - HALO, b * n_ctx), HALO)
            next0 = pl.multiple_of(jnp.minimum(base + tc, (b + 1) * n_ctx - HALO), HALO)
            win[b, 0:HALO, :] = jnp.where(j > 0, rc_ref[n, pl.ds(prev0, HALO), :], 0.0)
            win[b, HALO:HALO + tc, :] = rc_ref[n, pl.ds(pl.multiple_of(base, tc), tc), :]
            win[b, HALO + tc:2 * HALO + tc, :] = jnp.where(
                j < ctx_chunks - 1, rc_ref[n, pl.ds(next0, HALO), :], 0.0)

    def fill_from_x(win, n, j, cur_ref, prev_ref, next_ref):
        for b in range(nbatch):
            win[b, 0:HALO, :] = jnp.where(j > 0, prev_ref[b, n], 0.0)
            win[b, HALO:HALO + tc, :] = cur_ref[b, n]
            win[b, HALO + tc:2 * HALO + tc, :] = jnp.where(j < x_chunks - 1, next_ref[b, n], 0.0)

    def coefficients(d, j_ctx, j_x, x_refs, a_ref, b_ref):
        cols = slice(d * 2 * RG_BW, (d + 1) * 2 * RG_BW)

        def fill(n, win):
            pl.when(in_ctx)(lambda: fill_from_ctx(win, n, j_ctx))
            pl.when(jnp.logical_not(in_ctx))(lambda: fill_from_x(win, n, j_x, *x_refs))

        def body(g, p, win):
            n = g * COEF_UNROLL + p
            cw = cw_ref[n]
            hus = []
            for b in range(nbatch):
                y = cb_ref[n]
                for k in range(CONV_W):
                    lo = HALO + k - CONV_W // 2
                    y = y + win[b, lo:lo + tc, :] * cw[k:k + 1, :]
                hus.append(y)
            hu = jnp.concatenate(hus, axis=0)
            z = jnp.dot(hu.astype(BF16), gw_ref[n, :, cols], preferred_element_type=F32) + gb_ref[n, :, cols]
            t_r = jnp.tanh(z[:, :RG_BW])
            t_i = jnp.tanh(z[:, RG_BW:])
            lam = -lam_ref[n, d:d + 1, :]
            softplus = jnp.maximum(lam, 0.0) + jnp.log1p(jnp.exp(-jnp.abs(lam)))
            log_a = (t_r + 1.0) * ((-0.5 * RG_C) * softplus)
            a = jnp.exp(log_a)
            th = jnp.tanh(log_a)
            num = -2.0 * th
            root = jnp.where(num > 0.0, num * lax.rsqrt(num * (1.0 - th)), 0.0)
            coef = root * ((t_i + 1.0) * hu)
            for b in range(nbatch):
                rows = pl.ds(p * nbatch + b, tc, stride=SUBLANES)
                a_ref[g, rows, :] = a[b * tc:(b + 1) * tc]
                b_ref[g, rows, :] = coef[b * tc:(b + 1) * tc]

        def group(g, carry):
            for p in range(COEF_UNROLL):
                fill(g * COEF_UNROLL + p, win_ref.at[p])
            for p in range(COEF_UNROLL):
                body(g, p, win_ref.at[p])
            return carry

        lax.fori_loop(0, ngroups, group, 0)

    xf = c - ctx_chunks
    coefficients(0, c, xf, (rxf_ref, rxf_prev_ref, rxf_next_ref), af_ref, bf_ref)
    coefficients(1, ctx_chunks - 1 - c, x_chunks - 1 - xf, (rxb_ref, rxb_prev_ref, rxb_next_ref),
                 ab_ref, bb_ref)

    def scan_step(t, hs):
        out_f, out_b = [], []
        rows_f = pl.ds(pl.multiple_of(t * SUBLANES, SUBLANES), SUBLANES)
        rows_b = pl.ds(pl.multiple_of((tc - 1 - t) * SUBLANES, SUBLANES), SUBLANES)
        for g in range(ngroups):
            h = af_ref[g, rows_f, :] * hs[g] + bf_ref[g, rows_f, :]
            sf_ref[g, rows_f, :] = h
            out_f.append(h)
            h = ab_ref[g, rows_b, :] * hs[ngroups + g] + bb_ref[g, rows_b, :]
            sb_ref[g, rows_b, :] = h
            out_b.append(h)
        return tuple(out_f + out_b)

    hs = tuple(state_ref[d, g] for d in range(2) for g in range(ngroups))
    hs = lax.fori_loop(0, tc, scan_step, hs, unroll=SCAN_UNROLL)
    for d in range(2):
        for g in range(ngroups):
            state_ref[d, g] = hs[d * ngroups + g]

    @pl.when(jnp.logical_not(in_ctx))
    def _():
        def emit(g, carry):
            for p in range(COEF_UNROLL):
                for b in range(nbatch):
                    rows = pl.ds(p * nbatch + b, tc, stride=SUBLANES)
                    hf_ref[b, g * COEF_UNROLL + p] = sf_ref[g, rows, :].astype(hf_ref.dtype)
                    hb_ref[b, g * COEF_UNROLL + p] = sb_ref[g, rows, :].astype(hb_ref.dtype)
            return carry
        lax.fori_loop(0, ngroups, emit, 0)


def _rglru(rx, rc, n_ctx, conv_w, conv_b, gate_w, gate_b, lam):
    b, nb, s, _ = rx.shape
    tc = RG_CHUNK
    ctx_chunks = n_ctx // tc
    x_chunks = s // tc
    per_halo = tc // HALO
    gw = jnp.transpose(gate_w, (2, 3, 0, 1, 4)).reshape(nb, RG_BW, 4 * RG_BW).astype(BF16)
    gb = 0.5 * jnp.transpose(gate_b.reshape(2, 2, nb, RG_BW), (2, 0, 1, 3)).reshape(nb, 1, 4 * RG_BW)
    cw = 0.5 * jnp.transpose(conv_w.reshape(CONV_W, nb, RG_BW), (1, 0, 2))
    cb = 0.5 * conv_b.reshape(nb, 1, RG_BW)
    lam_s = jnp.transpose(lam.reshape(2, nb, RG_BW), (1, 0, 2))

    chunk_f = lambda c: jnp.maximum(c - ctx_chunks, 0)
    chunk_b = lambda c: jnp.minimum(x_chunks - 1 + ctx_chunks - c, x_chunks - 1)
    cur = lambda f: pl.BlockSpec((b, nb, tc, LANES), lambda c: (0, 0, f(c), 0))
    prev = lambda f: pl.BlockSpec((b, nb, HALO, LANES),
                                  lambda c: (0, 0, jnp.maximum(f(c) * per_halo - 1, 0), 0))
    nxt = lambda f: pl.BlockSpec((b, nb, HALO, LANES),
                                 lambda c: (0, 0, jnp.minimum((f(c) + 1) * per_halo, s // HALO - 1), 0))
    assert COEF_UNROLL * b == SUBLANES and nb % COEF_UNROLL == 0, (b, nb)
    ngroups = nb // COEF_UNROLL
    slab = pltpu.VMEM((ngroups, tc * SUBLANES, LANES), F32)
    window = pltpu.VMEM((COEF_UNROLL, b, tc + 2 * HALO, LANES), F32)
    state_shape = jax.ShapeDtypeStruct((b, nb, s, LANES), BF16)
    return pl.pallas_call(
        functools.partial(_rglru_kernel, n_ctx=n_ctx),
        grid=(ctx_chunks + x_chunks,),
        in_specs=[cur(chunk_f), prev(chunk_f), nxt(chunk_f), cur(chunk_b), prev(chunk_b), nxt(chunk_b),
                  _resident(rc.shape), _resident(cw.shape), _resident(cb.shape), _resident(gw.shape),
                  _resident(gb.shape), _resident(lam_s.shape)],
        out_specs=[cur(chunk_f), cur(chunk_b)],
        out_shape=[state_shape, state_shape],
        scratch_shapes=[window, slab, slab, slab, slab, slab, slab,
                        pltpu.VMEM((2, ngroups, SUBLANES, LANES), F32)],
        compiler_params=_cparams(1, 52),
        name="rglru",
    )(rx, rx, rx, rx, rx, rx, rc, cw, cb, gw, gb, lam_s)


def kernel(x, c, ctx, c_ctx, ada_w, ada_b, norm_g, ffn_w_gu, ffn_w_down, ab_w_in, ab_w_out, pool_w,
           pool_scale, na_rpb, rg_w_in, rg_conv_w, rg_conv_b, rg_gate_w, rg_gate_b, rg_lambda,
           rg_w_out, final_g):
    b, s, d = x.shape
    n_ctx = ctx.shape[1]
    depth = ada_w.shape[0]
    pool_width = pool_w.shape[1] * pool_w.shape[2]
    na_width = (ab_w_in.shape[-1] - pool_width) // 3
    d_rnn = rg_w_out.shape[1]
    na_scale = NA_HEAD_DIM ** -0.5

    mods = _mods(c, c_ctx, ada_w, ada_b)
    wgu = [[ffn_w_gu[i, k].astype(BF16) for k in range(2)] for i in range(depth)]
    wd = [[ffn_w_down[i, k].astype(BF16) for k in range(2)] for i in range(depth)]
    cflat = ctx.reshape(1, b * n_ctx, d)

    for i in range(depth):
        last = i == depth - 1
        mx = mods[i, :b]
        mc = mods[i, b:b + 1]
        g = norm_g[i]
        j = i // 2
        fg = final_g.reshape(1, d) if last else None
        if i % 2 == 0:
            win = ab_w_in[j].astype(BF16)
            wout = ab_w_out[j].astype(BF16)
            pw = pool_w[j].astype(BF16)
            ps = pool_scale[j].reshape(1, pool_width)
            splits = ((pool_width, 1.0, False), (na_width, na_scale, False), (na_width, 1.0, False),
                      (na_width, 1.0, False))
            dts = (F32, BF16, BF16, BF16)
            x, px, qx, kx, vx = _pre(x, mx, g, wgu[i][0], wd[i][0], win, splits, dts, f"pre{i}_x")
            cflat, pc, qc, kc, vc = _pre(cflat, mc, g, wgu[i][0], wd[i][0], win, splits, dts, f"pre{i}_c")
            per_b = lambda a: a.reshape(b, n_ctx, a.shape[-1])
            pc, qc, kc, vc = per_b(pc), per_b(qc), per_b(kc), per_b(vc)
            a_x = _pool(px, pw, ps, f"pool{i}_x")
            b_x = _nattn(qx, kx, vx, kc, vc, na_rpb[j])
            x = _post(x, mx, g, [a_x, b_x], wout, wgu[i][1], wd[i][1], fg, f"post{i}_x")
            if not last:
                a_c = _pool(pc, pw, ps, f"pool{i}_c")
                b_c = _ctx_attn(qc, kc, vc)
                flat = lambda a: a.reshape(1, b * n_ctx, a.shape[-1])
                cflat = _post(cflat, mc, g, [flat(a_c), flat(b_c)], wout, wgu[i][1], wd[i][1], None,
                              f"post{i}_c")
        else:
            win = rg_w_in[j].astype(BF16)
            wout = rg_w_out[j].astype(BF16)
            splits = ((d_rnn, 1.0, False), (d_rnn, 1.0, True))
            dts = (BF16, F32)
            x, gx, rx = _pre(x, mx, g, wgu[i][0], wd[i][0], win, splits, dts, f"pre{i}_x")
            cflat, gc, rc = _pre(cflat, mc, g, wgu[i][0], wd[i][0], win, splits, dts, f"pre{i}_c")
            hf, hb = _rglru(rx, rc[0], n_ctx, rg_conv_w[j], rg_conv_b[j], rg_gate_w[j], rg_gate_b[j],
                            rg_lambda[j])
            x = _post(x, mx, g, [gx, hf, hb], wout, wgu[i][1], wd[i][1], fg, f"post{i}_x",
                      gated_scan=True)
            if not last:
                raise NotImplementedError("context output of an RG-LRU layer is not needed at this depth")
    return x
```

```python
import functools

import numpy as np
import jax
import jax.numpy as jnp
from jax import lax
from jax.experimental import pallas as pl
from jax.experimental.pallas import tpu as pltpu

F32 = jnp.float32
BF16 = jnp.bfloat16

N_MOD = 9
RMS_EPS = 1e-6
GRID_W = 64
POOL_WINDOWS = (2, 4, 8, 16)
NA_HEAD_DIM = 64
NB_ROWS = 8
NB_COLS = 16
RG_BW = 128
CONV_W = 4
RG_C = 8.0

LANES = 128
SUBLANES = 8
MXU_DIM = 256
VMEM_BYTES = 64 * 1024 * 1024

HEADS_PER_GROUP = MXU_DIM // NA_HEAD_DIM
TOKEN_TILE = 512
FF_CHUNK = MXU_DIM


def _cparams(n_axes, vmem_mb):
    return pltpu.CompilerParams(
        dimension_semantics=("arbitrary",) * n_axes,
        vmem_limit_bytes=min(vmem_mb * 1024 * 1024, VMEM_BYTES - 8 * 1024 * 1024),
    )


def _resident(shape):
    nd = len(shape)
    return pl.BlockSpec(shape, lambda *_: (0,) * nd, pipeline_mode=pl.Buffered(1))


def _silu(x):
    return x * jax.nn.sigmoid(x)


def _gelu_tanh(x):
    c = np.sqrt(2.0 / np.pi).astype(np.float32)
    return 0.5 * x * (1.0 + jnp.tanh(c * (x + 0.044715 * (x * x * x))))


def _rms_mod(x, g, shift, scale):
    ms = jnp.mean(x * x, axis=-1, keepdims=True)
    y = (x * lax.rsqrt(ms + RMS_EPS)) * g
    return y * (1.0 + scale) + shift


def _swiglu(h, wgu_ref, wd_ref, act_ref):
    d_ff = wd_ref.shape[0]
    for c in range(d_ff // FF_CHUNK):
        lo = c * FF_CHUNK
        ug = jnp.dot(h, wgu_ref[:, lo:lo + FF_CHUNK], preferred_element_type=F32)
        uu = jnp.dot(h, wgu_ref[:, d_ff + lo:d_ff + lo + FF_CHUNK], preferred_element_type=F32)
        act_ref[:, lo:lo + FF_CHUNK] = (_silu(ug) * uu).astype(BF16)
    return jnp.dot(act_ref[...], wd_ref[...], preferred_element_type=F32)


WEIGHT_CHUNK_ROWS = 128


def _hbm():
    return pl.BlockSpec(memory_space=pl.ANY)


def _load_weights_bf16(jobs, stage_ref, sem):
    rows = WEIGHT_CHUNK_ROWS
    for src, dst in jobs:
        n, cols = dst.shape[0] // rows, dst.shape[1]

        def chunk_copy(i, slot, src=src, cols=cols):
            return pltpu.make_async_copy(
                src.at[pl.ds(pl.multiple_of(i * rows, rows), rows), :],
                stage_ref.at[slot, :, pl.ds(0, cols)], sem.at[slot])

        chunk_copy(0, 0).start()

        def body(i, carry, n=n, cols=cols, dst=dst, chunk_copy=chunk_copy):
            slot = i % 2

            @pl.when(i + 1 < n)
            def _():
                chunk_copy(i + 1, 1 - slot).start()

            chunk_copy(i, slot).wait()
            dst[pl.ds(pl.multiple_of(i * rows, rows), rows), :] = stage_ref[slot, :, 0:cols].astype(BF16)
            return carry

        lax.fori_loop(0, n, body, 0)


def _weight_scratch(shapes):
    width = max(s[1] for s in shapes)
    return ([pltpu.VMEM(s, BF16) for s in shapes]
            + [pltpu.VMEM((2, WEIGHT_CHUNK_ROWS, width), F32), pltpu.SemaphoreType.DMA((2,))])


def _mods_kernel(c_ref, w_ref, b_ref, o_ref):
    h = _silu(c_ref[...]).astype(BF16)
    w = w_ref[0].astype(BF16)
    o_ref[0] = jnp.dot(h, w, preferred_element_type=F32) + b_ref[0]


def _mods(c, c_ctx, ada_w, ada_b):
    depth, d, n = ada_w.shape
    b = c.shape[0]
    rows = jnp.zeros((SUBLANES, d), F32).at[:b].set(c).at[b].set(c_ctx)
    tn = n // 4
    out = pl.pallas_call(
        _mods_kernel,
        grid=(depth, n // tn),
        in_specs=[
            pl.BlockSpec((SUBLANES, d), lambda i, j: (0, 0)),
            pl.BlockSpec((1, d, tn), lambda i, j: (i, 0, j)),
            pl.BlockSpec((1, 1, tn), lambda i, j: (i, 0, j)),
        ],
        out_specs=pl.BlockSpec((1, SUBLANES, tn), lambda i, j: (i, 0, j)),
        out_shape=jax.ShapeDtypeStruct((depth, SUBLANES, n), F32),
        compiler_params=_cparams(2, 40),
        name="mods",
    )(rows, ada_w, ada_b.reshape(depth, 1, n))
    return out.reshape(depth, SUBLANES, N_MOD, d)


class _Streams:
    def __init__(self, arrays, tm):
        self.tm = tm
        self.tiles = [a.shape[0] * (a.shape[1] // tm) for a in arrays]
        self.per_batch = [a.shape[1] // tm for a in arrays]
        self.starts = [sum(self.tiles[:s]) for s in range(len(arrays))]

    @property
    def grid(self):
        return (sum(self.tiles),)

    def _pos(self, s, step):
        local = jnp.clip(step - self.starts[s], 0, self.tiles[s] - 1)
        return local // self.per_batch[s], local % self.per_batch[s]

    def tok(self, s, width):
        return pl.BlockSpec((1, self.tm, width), lambda i: (*self._pos(s, i), 0))

    def slab(self, s, n_slabs):
        def index(i):
            b, t = self._pos(s, i)
            return (b, 0, t, 0)
        return pl.BlockSpec((1, n_slabs, self.tm, LANES), index)

    def per_batch_rows(self, s, rows, width):
        return pl.BlockSpec((1, rows, width), lambda i: (self._pos(s, i)[0], 0, 0))

    def active(self, s, step):
        return (step >= self.starts[s]) & (step < self.starts[s] + self.tiles[s])


def _select_stream(active, refs, index=0):
    val = refs[-1][index]
    for a, r in zip(reversed(active[:-1]), reversed(refs[:-1])):
        val = jnp.where(a, r[index], val)
    return val


def _pre_kernel(*refs, splits, n_streams, starts, tiles, w_index):
    ns = n_streams
    x_refs, mod_refs = refs[:ns], refs[ns:2 * ns]
    g_ref, wgu_hbm, wd_hbm, win_hbm = refs[2 * ns:2 * ns + 4]
    n_out = 1 + len(splits)
    out_refs = refs[2 * ns + 4:2 * ns + 4 + ns * n_out]
    act_ref, wgu_ref, wd_ref, win_ref, stage_ref, sem = refs[2 * ns + 4 + ns * n_out:]
    step = pl.program_id(0)
    layer, mixer = w_index

    @pl.when(step == 0)
    def _():
        _load_weights_bf16([(wgu_hbm.at[layer, 0], wgu_ref), (wd_hbm.at[layer, 0], wd_ref),
                            (win_hbm.at[mixer], win_ref)], stage_ref, sem)

    active = [(step >= starts[s]) & (step < starts[s] + tiles[s]) for s in range(ns)]
    x = _select_stream(active, x_refs)
    mod = _select_stream(active, mod_refs)
    h = _rms_mod(x, g_ref[0:1, :], mod[0:1, :], mod[1:2, :]).astype(BF16)
    y = _swiglu(h, wgu_ref, wd_ref, act_ref)
    x = x + (0.5 * mod[2:3, :]) * y
    h = _rms_mod(x, g_ref[1:2, :], mod[3:4, :], mod[4:5, :]).astype(BF16)
    us = []
    off = 0
    for width, scale, _ in splits:
        u = jnp.dot(h, win_ref[:, off:off + width], preferred_element_type=F32)
        us.append(u if scale == 1.0 else u * scale)
        off += width

    def write(s):
        outs = out_refs[s * n_out:(s + 1) * n_out]
        outs[0][0] = x
        for o_ref, u, (width, _, slab) in zip(outs[1:], us, splits):
            if slab:
                for j in range(width // LANES):
                    o_ref[0, j] = u[:, j * LANES:(j + 1) * LANES].astype(o_ref.dtype)
            else:
                o_ref[0] = u.astype(o_ref.dtype)

    for s in range(ns):
        pl.when(active[s])(functools.partial(write, s))


def _pre(xs, mods, g, wgu, wd, win, w_index, splits, dtypes, name):
    d = xs[0].shape[-1]
    tm = min([TOKEN_TILE] + [a.shape[1] for a in xs])
    st = _Streams(xs, tm)
    ns = len(xs)
    in_specs = [st.tok(s, d) for s in range(ns)] + [st.per_batch_rows(s, N_MOD, d) for s in range(ns)]
    in_specs += [_resident(g.shape), _hbm(), _hbm(), _hbm()]
    out_shape, out_specs = [], []
    for s, a in enumerate(xs):
        n, t, _ = a.shape
        out_shape.append(jax.ShapeDtypeStruct((n, t, d), F32))
        out_specs.append(st.tok(s, d))
        for (w, _, is_slab), dt in zip(splits, dtypes):
            if is_slab:
                out_shape.append(jax.ShapeDtypeStruct((n, w // LANES, t, LANES), dt))
                out_specs.append(st.slab(s, w // LANES))
            else:
                out_shape.append(jax.ShapeDtypeStruct((n, t, w), dt))
                out_specs.append(st.tok(s, w))
    w_shapes = [wgu.shape[-2:], wd.shape[-2:], win.shape[-2:]]
    outs = pl.pallas_call(
        functools.partial(_pre_kernel, splits=splits, n_streams=ns, starts=st.starts, tiles=st.tiles,
                          w_index=w_index),
        grid=st.grid,
        in_specs=in_specs,
        out_specs=out_specs,
        out_shape=out_shape,
        scratch_shapes=[pltpu.VMEM((tm, wd.shape[-2]), BF16)] + _weight_scratch(w_shapes),
        compiler_params=_cparams(1, 56),
        name=name,
    )(*xs, *mods, g, wgu, wd, win)
    n_out = 1 + len(splits)
    return [outs[s * n_out:(s + 1) * n_out] for s in range(ns)]


def _post_kernel(*refs, n_streams, n_parts, final, gated_scan, starts, tiles, w_index):
    ns = n_streams
    x_refs, mod_refs, g_ref = refs[:ns], refs[ns:2 * ns], refs[2 * ns]
    pos = 2 * ns + 1
    part_refs = [refs[pos + s * n_parts:pos + (s + 1) * n_parts] for s in range(ns)]
    pos += ns * n_parts
    wout_hbm, wgu_hbm, wd_hbm = refs[pos:pos + 3]
    pos += 3
    if final:
        gf_ref = refs[pos]
        pos += 1
    out_refs = refs[pos:pos + ns]
    pos += ns
    act_ref = refs[pos]
    pos += 1
    if gated_scan:
        mix_ref = refs[pos]
        pos += 1
    wout_ref, wgu_ref, wd_ref, stage_ref, sem = refs[pos:]
    step = pl.program_id(0)
    layer, mixer = w_index

    @pl.when(step == 0)
    def _():
        _load_weights_bf16([(wout_hbm.at[mixer], wout_ref), (wgu_hbm.at[layer, 1], wgu_ref),
                            (wd_hbm.at[layer, 1], wd_ref)], stage_ref, sem)

    active = [(step >= starts[s]) & (step < starts[s] + tiles[s]) for s in range(ns)]
    if gated_scan:
        gate_ref, hf_ref, hb_ref = part_refs[0]
        for j in range(hf_ref.shape[1]):
            gate = gate_ref[0, :, j * LANES:(j + 1) * LANES].astype(F32)
            hsum = hf_ref[0, j].astype(F32) + hb_ref[0, j].astype(F32)
            mix_ref[:, j * LANES:(j + 1) * LANES] = (_gelu_tanh(gate) * hsum).astype(BF16)
        acc = jnp.dot(mix_ref[...], wout_ref[...], preferred_element_type=F32)
    else:
        acc = None
        off = 0
        for j in range(n_parts):
            y = _select_stream(active, [p[j] for p in part_refs])
            w = y.shape[-1]
            part = jnp.dot(y, wout_ref[off:off + w, :], preferred_element_type=F32)
            acc = part if acc is None else acc + part
            off += w
    mod = _select_stream(active, mod_refs)
    x = _select_stream(active, x_refs) + mod[5:6, :] * acc
    h = _rms_mod(x, g_ref[2:3, :], mod[6:7, :], mod[7:8, :]).astype(BF16)
    y = _swiglu(h, wgu_ref, wd_ref, act_ref)
    x = x + (0.5 * mod[8:9, :]) * y
    if final:
        ms = jnp.mean(x * x, axis=-1, keepdims=True)
        x = (x * lax.rsqrt(ms + RMS_EPS)) * gf_ref[...]

    def write(s):
        out_refs[s][0] = x

    for s in range(ns):
        pl.when(active[s])(functools.partial(write, s))


def _post(xs, mods, g, parts, wout, wgu, wd, w_index, final_g, name, gated_scan=False):
    d = xs[0].shape[-1]
    tm = min([TOKEN_TILE] + [a.shape[1] for a in xs])
    st = _Streams(xs, tm)
    ns = len(xs)
    args = [*xs, *mods, g]
    in_specs = [st.tok(s, d) for s in range(ns)] + [st.per_batch_rows(s, N_MOD, d) for s in range(ns)]
    in_specs.append(_resident(g.shape))
    for s in range(ns):
        for p in parts[s]:
            args.append(p)
            in_specs.append(st.tok(s, p.shape[-1]) if p.ndim == 3 else st.slab(s, p.shape[1]))
    args += [wout, wgu, wd]
    in_specs += [_hbm(), _hbm(), _hbm()]
    if final_g is not None:
        args.append(final_g)
        in_specs.append(_resident(final_g.shape))
    scratch = [pltpu.VMEM((tm, wd.shape[-2]), BF16)]
    if gated_scan:
        scratch.append(pltpu.VMEM((tm, wout.shape[-2]), BF16))
    scratch += _weight_scratch([wout.shape[-2:], wgu.shape[-2:], wd.shape[-2:]])
    return pl.pallas_call(
        functools.partial(_post_kernel, n_streams=ns, n_parts=len(parts[0]), final=final_g is not None,
                          gated_scan=gated_scan, starts=st.starts, tiles=st.tiles, w_index=w_index),
        grid=st.grid,
        in_specs=in_specs,
        out_specs=[st.tok(s, d) for s in range(ns)],
        out_shape=[jax.ShapeDtypeStruct(a.shape, F32) for a in xs],
        scratch_shapes=scratch,
        compiler_params=_cparams(1, 56),
        name=name,
    )(*args)


POOL_PAD = SUBLANES * 2


def _pool_kernel(p_ref, w_ref, s_ref, o_ref, pad_ref):
    t = p_ref.shape[1]
    gw = w_ref.shape[-1]
    zeros = jnp.zeros((POOL_PAD, gw), F32)
    pad_ref[0:POOL_PAD, :] = zeros
    pad_ref[POOL_PAD + t:POOL_PAD + t + POOL_PAD, :] = zeros
    pos = lax.broadcasted_iota(jnp.int32, (t, gw), 0)
    for gi, win in enumerate(POOL_WINDOWS):
        half = win // 2
        u = p_ref[0, :, gi * gw:(gi + 1) * gw]
        pad_ref[POOL_PAD:POOL_PAD + t, :] = u
        wsum = None
        for k in range(-half, half):
            piece = pad_ref[POOL_PAD + k:POOL_PAD + k + t, :]
            wsum = piece if wsum is None else wsum + piece
        cnt = (jnp.minimum(pos + half, t) - jnp.maximum(pos - half, 0)).astype(F32)
        y = (wsum / cnt - u).astype(BF16)
        z = jnp.dot(y, w_ref[gi], preferred_element_type=F32)
        o_ref[0, :, gi * gw:(gi + 1) * gw] = (z * s_ref[:, gi * gw:(gi + 1) * gw]).astype(o_ref.dtype)


def _pool(p, pool_w, pool_scale, name):
    n, t, w = p.shape
    gw = pool_w.shape[-1]
    return pl.pallas_call(
        _pool_kernel,
        grid=(n,),
        in_specs=[
            pl.BlockSpec((1, t, w), lambda b: (b, 0, 0)),
            _resident(pool_w.shape), _resident(pool_scale.shape),
        ],
        out_specs=pl.BlockSpec((1, t, w), lambda b: (b, 0, 0)),
        out_shape=jax.ShapeDtypeStruct((n, t, w), BF16),
        scratch_shapes=[pltpu.VMEM((t + 2 * POOL_PAD, gw), F32)],
        compiler_params=_cparams(1, 48),
        name=name,
    )(p, pool_w, pool_scale)


def _head_block_mask(shape):
    r = lax.broadcasted_iota(jnp.int32, shape, 0) // NA_HEAD_DIM
    c = lax.broadcasted_iota(jnp.int32, shape, 1) // NA_HEAD_DIM
    return r == c


def _attend(q_blk, key_parts, val_parts, bias_parts):
    nq, g = q_blk.shape
    heads = g // NA_HEAD_DIM
    qbd = jnp.where(_head_block_mask((heads * nq, g)),
                    jnp.concatenate([q_blk] * heads, axis=0), jnp.zeros((), BF16))
    nt = (((1,), (1,)), ((), ()))
    scores = []
    for keys, bias in zip(key_parts, bias_parts):
        s = lax.dot_general(keys, qbd, nt, preferred_element_type=F32)
        scores.append(s if bias is None else s + bias)
    m = functools.reduce(jnp.maximum, [jnp.max(s, axis=0, keepdims=True) for s in scores])
    ps = [jnp.exp(s - m) for s in scores]
    l = functools.reduce(jnp.add, [jnp.sum(p, axis=0, keepdims=True) for p in ps])
    inv = 1.0 / l
    tn = (((0,), (0,)), ((), ()))
    r = None
    for p, vals in zip(ps, val_parts):
        part = lax.dot_general((p * inv).astype(BF16), vals, tn, preferred_element_type=F32)
        r = part if r is None else r + part
    lane_head = lax.broadcasted_iota(jnp.int32, (nq, g), 1) // NA_HEAD_DIM
    out = jnp.zeros((nq, g), F32)
    for h in range(heads):
        out = jnp.where(lane_head == h, r[h * nq:(h + 1) * nq, :], out)
    return out


NATTN_UNROLL = 8


def _nattn_kernel(q_ref, k_ref, v_ref, kc_ref, vc_ref, bias_ref, o_ref):
    s = q_ref.shape[1]
    rows = s // GRID_W
    kh = min(NB_ROWS, rows)
    kc = kc_ref[0]
    vc = vc_ref[0]

    def body(i, carry):
        r0 = jnp.clip(i - kh // 2, 0, rows - kh)
        q0 = pl.multiple_of(i * GRID_W, GRID_W)
        k0 = pl.multiple_of(r0 * GRID_W, GRID_W)
        q_blk = q_ref[0, pl.ds(q0, GRID_W), :]
        keys = k_ref[0, pl.ds(k0, kh * GRID_W), :]
        vals = v_ref[0, pl.ds(k0, kh * GRID_W), :]
        bias = bias_ref[0, pl.ds(r0 - i + (NB_ROWS - 1), kh), :, :]
        bias = bias.reshape(kh * GRID_W, bias.shape[-1])
        out = _attend(q_blk, [keys, kc], [vals, vc], [bias, None])
        o_ref[0, pl.ds(q0, GRID_W), :] = out.astype(o_ref.dtype)
        return carry

    lax.fori_loop(0, rows, body, 0, unroll=NATTN_UNROLL)


def _ctx_attn_kernel(q_ref, kc_ref, vc_ref, o_ref):
    t = q_ref.shape[1]
    kc = kc_ref[0]
    vc = vc_ref[0]
    for i in range(t // GRID_W):
        q_blk = q_ref[0, i * GRID_W:(i + 1) * GRID_W, :]
        out = _attend(q_blk, [kc], [vc], [None])
        o_ref[0, i * GRID_W:(i + 1) * GRID_W, :] = out.astype(o_ref.dtype)


def _bias_table(rpb):
    h = rpb.shape[0]
    kcol = np.arange(GRID_W)[:, None]
    qcol = np.arange(GRID_W)[None, :]
    qstart = np.clip(qcol - NB_COLS // 2, 0, GRID_W - NB_COLS)
    mask = (kcol >= qstart) & (kcol < qstart + NB_COLS)
    idx = np.clip(kcol - qcol + NB_COLS - 1, 0, 2 * NB_COLS - 2)
    onehot = (idx[None] == np.arange(rpb.shape[-1])[:, None, None]).astype(np.float32)
    t = jnp.einsum("hrj,jkq->hrkq", rpb.astype(F32), onehot, precision=lax.Precision.HIGHEST)
    t = jnp.where(mask[None, None], t, -jnp.inf)
    t = t.reshape(h // HEADS_PER_GROUP, HEADS_PER_GROUP, t.shape[1], GRID_W, GRID_W)
    t = jnp.transpose(t, (0, 2, 3, 1, 4))
    return t.reshape(h // HEADS_PER_GROUP, t.shape[1], GRID_W, HEADS_PER_GROUP * GRID_W)


def _nattn(q, k, v, kc, vc, rpb):
    b, s, w = q.shape
    ctx = kc.shape[1]
    g = HEADS_PER_GROUP * NA_HEAD_DIM
    bias = _bias_table(rpb)
    seq = lambda t: pl.BlockSpec((1, t, g), lambda i, j: (i, 0, j))
    return pl.pallas_call(
        _nattn_kernel,
        grid=(b, w // g),
        in_specs=[seq(s), seq(s), seq(s), seq(ctx), seq(ctx),
                  pl.BlockSpec((1,) + bias.shape[1:], lambda i, j: (j, 0, 0, 0))],
        out_specs=seq(s),
        out_shape=jax.ShapeDtypeStruct((b, s, w), BF16),
        compiler_params=_cparams(2, 48),
        name="nattn",
    )(q, k, v, kc, vc, bias)


def _ctx_attn(q, kc, vc):
    b, t, w = q.shape
    g = HEADS_PER_GROUP * NA_HEAD_DIM
    seq = pl.BlockSpec((1, t, g), lambda i, j: (i, 0, j))
    return pl.pallas_call(
        _ctx_attn_kernel,
        grid=(b, w // g),
        in_specs=[seq, seq, seq],
        out_specs=seq,
        out_shape=jax.ShapeDtypeStruct((b, t, w), BF16),
        compiler_params=_cparams(2, 32),
        name="ctx_attn",
    )(q, kc, vc)


RG_CHUNK = 128
HALO = SUBLANES
SCAN_UNROLL = 8
COEF_UNROLL = 2


def _rglru_kernel(rxf_ref, rxf_prev_ref, rxf_next_ref, rxb_ref, rxb_prev_ref, rxb_next_ref, rc_ref,
                  cw_ref, cb_ref, gw_ref, gb_ref, lam_ref, hf_ref, hb_ref,
                  win_ref, af_ref, bf_ref, ab_ref, bb_ref, sf_ref, sb_ref, state_ref, *, n_ctx):
    nbatch, nblk, tc, _ = rxf_ref.shape
    ngroups = nblk // COEF_UNROLL
    ctx_chunks = n_ctx // tc
    x_chunks = pl.num_programs(0) - ctx_chunks
    c = pl.program_id(0)
    in_ctx = c < ctx_chunks

    @pl.when(c == 0)
    def _():
        state_ref[...] = jnp.zeros(state_ref.shape, F32)

    def fill_from_ctx(win, n, j):
        for b in range(nbatch):
            base = b * n_ctx + j * tc
            prev0 = pl.multiple_of(jnp.maximum(base - HALO, b * n_ctx), HALO)
            next0 = pl.multiple_of(jnp.minimum(base + tc, (b + 1) * n_ctx - HALO), HALO)
            win[b, 0:HALO, :] = jnp.where(j > 0, rc_ref[n, pl.ds(prev0, HALO), :], 0.0)
            win[b, HALO:HALO + tc, :] = rc_ref[n, pl.ds(pl.multiple_of(base, tc), tc), :]
            win[b, HALO + tc:2 * HALO + tc, :] = jnp.where(
                j < ctx_chunks - 1, rc_ref[n, pl.ds(next0, HALO), :], 0.0)

    def fill_from_x(win, n, j, cur_ref, prev_ref, next_ref):
        for b in range(nbatch):
            win[b, 0:HALO, :] = jnp.where(j > 0, prev_ref[b, n], 0.0)
            win[b, HALO:HALO + tc, :] = cur_ref[b, n]
            win[b, HALO + tc:2 * HALO + tc, :] = jnp.where(j < x_chunks - 1, next_ref[b, n], 0.0)

    def coefficients(d, j_ctx, j_x, x_refs, a_ref, b_ref):
        cols = slice(d * 2 * RG_BW, (d + 1) * 2 * RG_BW)

        def fill(n, win):
            pl.when(in_ctx)(lambda: fill_from_ctx(win, n, j_ctx))
            pl.when(jnp.logical_not(in_ctx))(lambda: fill_from_x(win, n, j_x, *x_refs))

        def body(g, p, win):
            n = g * COEF_UNROLL + p
            cw = cw_ref[n]
            hus = []
            for b in range(nbatch):
                y = cb_ref[n]
                for k in range(CONV_W):
                    lo = HALO + k - CONV_W // 2
                    y = y + win[b, lo:lo + tc, :] * cw[k:k + 1, :]
                hus.append(y)
            hu = jnp.concatenate(hus, axis=0)
            z = jnp.dot(hu.astype(BF16), gw_ref[n, :, cols], preferred_element_type=F32) + gb_ref[n, :, cols]
            t_r = jnp.tanh(z[:, :RG_BW])
            t_i = jnp.tanh(z[:, RG_BW:])
            lam = -lam_ref[n, d:d + 1, :]
            softplus = jnp.maximum(lam, 0.0) + jnp.log1p(jnp.exp(-jnp.abs(lam)))
            log_a = (t_r + 1.0) * ((-0.5 * RG_C) * softplus)
            a = jnp.exp(log_a)
            th = jnp.tanh(log_a)
            num = -2.0 * th
            root = jnp.where(num > 0.0, num * lax.rsqrt(num * (1.0 - th)), 0.0)
            coef = root * ((t_i + 1.0) * hu)
            for b in range(nbatch):
                rows = pl.ds(p * nbatch + b, tc, stride=SUBLANES)
                a_ref[g, rows, :] = a[b * tc:(b + 1) * tc]
                b_ref[g, rows, :] = coef[b * tc:(b + 1) * tc]

        def group(g, carry):
            for p in range(COEF_UNROLL):
                fill(g * COEF_UNROLL + p, win_ref.at[p])
            for p in range(COEF_UNROLL):
                body(g, p, win_ref.at[p])
            return carry

        lax.fori_loop(0, ngroups, group, 0)

    xf = c - ctx_chunks
    coefficients(0, c, xf, (rxf_ref, rxf_prev_ref, rxf_next_ref), af_ref, bf_ref)
    coefficients(1, ctx_chunks - 1 - c, x_chunks - 1 - xf, (rxb_ref, rxb_prev_ref, rxb_next_ref),
                 ab_ref, bb_ref)

    def scan_step(t, hs):
        out_f, out_b = [], []
        rows_f = pl.ds(pl.multiple_of(t * SUBLANES, SUBLANES), SUBLANES)
        rows_b = pl.ds(pl.multiple_of((tc - 1 - t) * SUBLANES, SUBLANES), SUBLANES)
        for g in range(ngroups):
            h = af_ref[g, rows_f, :] * hs[g] + bf_ref[g, rows_f, :]
            sf_ref[g, rows_f, :] = h
            out_f.append(h)
            h = ab_ref[g, rows_b, :] * hs[ngroups + g] + bb_ref[g, rows_b, :]
            sb_ref[g, rows_b, :] = h
            out_b.append(h)
        return tuple(out_f + out_b)

    hs = tuple(state_ref[d, g] for d in range(2) for g in range(ngroups))
    hs = lax.fori_loop(0, tc, scan_step, hs, unroll=SCAN_UNROLL)
    for d in range(2):
        for g in range(ngroups):
            state_ref[d, g] = hs[d * ngroups + g]

    @pl.when(jnp.logical_not(in_ctx))
    def _():
        def emit(g, carry):
            for p in range(COEF_UNROLL):
                for b in range(nbatch):
                    rows = pl.ds(p * nbatch + b, tc, stride=SUBLANES)
                    hf_ref[b, g * COEF_UNROLL + p] = sf_ref[g, rows, :].astype(hf_ref.dtype)
                    hb_ref[b, g * COEF_UNROLL + p] = sb_ref[g, rows, :].astype(hb_ref.dtype)
            return carry
        lax.fori_loop(0, ngroups, emit, 0)


def _rglru(rx, rc, n_ctx, conv_w, conv_b, gate_w, gate_b, lam):
    b, nb, s, _ = rx.shape
    tc = RG_CHUNK
    ctx_chunks = n_ctx // tc
    x_chunks = s // tc
    per_halo = tc // HALO
    gw = jnp.transpose(gate_w, (2, 3, 0, 1, 4)).reshape(nb, RG_BW, 4 * RG_BW).astype(BF16)
    gb = 0.5 * jnp.transpose(gate_b.reshape(2, 2, nb, RG_BW), (2, 0, 1, 3)).reshape(nb, 1, 4 * RG_BW)
    cw = 0.5 * jnp.transpose(conv_w.reshape(CONV_W, nb, RG_BW), (1, 0, 2))
    cb = 0.5 * conv_b.reshape(nb, 1, RG_BW)
    lam_s = jnp.transpose(lam.reshape(2, nb, RG_BW), (1, 0, 2))

    chunk_f = lambda c: jnp.maximum(c - ctx_chunks, 0)
    chunk_b = lambda c: jnp.minimum(x_chunks - 1 + ctx_chunks - c, x_chunks - 1)
    cur = lambda f: pl.BlockSpec((b, nb, tc, LANES), lambda c: (0, 0, f(c), 0))
    prev = lambda f: pl.BlockSpec((b, nb, HALO, LANES),
                                  lambda c: (0, 0, jnp.maximum(f(c) * per_halo - 1, 0), 0))
    nxt = lambda f: pl.BlockSpec((b, nb, HALO, LANES),
                                 lambda c: (0, 0, jnp.minimum((f(c) + 1) * per_halo, s // HALO - 1), 0))
    assert COEF_UNROLL * b == SUBLANES and nb % COEF_UNROLL == 0, (b, nb)
    ngroups = nb // COEF_UNROLL
    slab = pltpu.VMEM((ngroups, tc * SUBLANES, LANES), F32)
    window = pltpu.VMEM((COEF_UNROLL, b, tc + 2 * HALO, LANES), F32)
    state_shape = jax.ShapeDtypeStruct((b, nb, s, LANES), BF16)
    return pl.pallas_call(
        functools.partial(_rglru_kernel, n_ctx=n_ctx),
        grid=(ctx_chunks + x_chunks,),
        in_specs=[cur(chunk_f), prev(chunk_f), nxt(chunk_f), cur(chunk_b), prev(chunk_b), nxt(chunk_b),
                  _resident(rc.shape), _resident(cw.shape), _resident(cb.shape), _resident(gw.shape),
                  _resident(gb.shape), _resident(lam_s.shape)],
        out_specs=[cur(chunk_f), cur(chunk_b)],
        out_shape=[state_shape, state_shape],
        scratch_shapes=[window, slab, slab, slab, slab, slab, slab,
                        pltpu.VMEM((2, ngroups, SUBLANES, LANES), F32)],
        compiler_params=_cparams(1, 52),
        name="rglru",
    )(rx, rx, rx, rx, rx, rx, rc, cw, cb, gw, gb, lam_s)


def kernel(x, c, ctx, c_ctx, ada_w, ada_b, norm_g, ffn_w_gu, ffn_w_down, ab_w_in, ab_w_out, pool_w,
           pool_scale, na_rpb, rg_w_in, rg_conv_w, rg_conv_b, rg_gate_w, rg_gate_b, rg_lambda,
           rg_w_out, final_g):
    b, s, d = x.shape
    n_ctx = ctx.shape[1]
    depth = ada_w.shape[0]
    pool_width = pool_w.shape[1] * pool_w.shape[2]
    na_width = (ab_w_in.shape[-1] - pool_width) // 3
    d_rnn = rg_w_out.shape[1]
    na_scale = NA_HEAD_DIM ** -0.5

    mods = _mods(c, c_ctx, ada_w, ada_b)
    cflat = ctx.reshape(1, b * n_ctx, d)

    for i in range(depth):
        last = i == depth - 1
        mx = mods[i, :b]
        mc = mods[i, b:b + 1]
        g = norm_g[i]
        j = i // 2
        fg = final_g.reshape(1, d) if last else None
        if i % 2 == 0:
            pw = pool_w[j].astype(BF16)
            ps = pool_scale[j].reshape(1, pool_width)
            splits = ((pool_width, 1.0, False), (na_width, na_scale, False), (na_width, 1.0, False),
                      (na_width, 1.0, False))
            dts = (F32, BF16, BF16, BF16)
            ((x, px, qx, kx, vx),) = _pre(
                [x], [mx], g, ffn_w_gu, ffn_w_down, ab_w_in, (i, j), splits, dts, f"pre{i}_x")
            ((cflat, pc, qc, kc, vc),) = _pre(
                [cflat], [mc], g, ffn_w_gu, ffn_w_down, ab_w_in, (i, j), splits, dts, f"pre{i}_c")
            per_b = lambda a: a.reshape(b, n_ctx, a.shape[-1])
            pc, qc, kc, vc = per_b(pc), per_b(qc), per_b(kc), per_b(vc)
            a_x = _pool(px, pw, ps, f"pool{i}_x")
            b_x = _nattn(qx, kx, vx, kc, vc, na_rpb[j])
            if last:
                (x,) = _post([x], [mx], g, [[a_x, b_x]], ab_w_out, ffn_w_gu, ffn_w_down, (i, j), fg,
                             f"post{i}")
            else:
                a_c = _pool(pc, pw, ps, f"pool{i}_c")
                b_c = _ctx_attn(qc, kc, vc)
                flat = lambda a: a.reshape(1, b * n_ctx, a.shape[-1])
                (x,) = _post([x], [mx], g, [[a_x, b_x]], ab_w_out, ffn_w_gu, ffn_w_down, (i, j), fg,
                             f"post{i}_x")
                (cflat,) = _post([cflat], [mc], g, [[flat(a_c), flat(b_c)]], ab_w_out, ffn_w_gu,
                                 ffn_w_down, (i, j), fg, f"post{i}_c")
        else:
            splits = ((d_rnn, 1.0, False), (d_rnn, 1.0, True))
            dts = (BF16, F32)
            ((x, gx, rx),) = _pre(
                [x], [mx], g, ffn_w_gu, ffn_w_down, rg_w_in, (i, j), splits, dts, f"pre{i}_x")
            ((cflat, _, rc),) = _pre(
                [cflat], [mc], g, ffn_w_gu, ffn_w_down, rg_w_in, (i, j), splits, dts, f"pre{i}_c")
            hf, hb = _rglru(rx, rc[0], n_ctx, rg_conv_w[j], rg_conv_b[j], rg_gate_w[j], rg_gate_b[j],
                            rg_lambda[j])
            if not last:
                raise NotImplementedError("context output of an RG-LRU layer is not needed at this depth")
            (x,) = _post([x], [mx], g, [[gx, hf, hb]], rg_w_out, ffn_w_gu, ffn_w_down, (i, j), fg,
                         f"post{i}", gated_scan=True)
    return x
```

```python
import functools

import numpy as np
import jax
import jax.numpy as jnp
from jax import lax
from jax.experimental import pallas as pl
from jax.experimental.pallas import tpu as pltpu

F32 = jnp.float32
BF16 = jnp.bfloat16

N_MOD = 9
RMS_EPS = 1e-6
GRID_W = 64
POOL_WINDOWS = (2, 4, 8, 16)
NA_HEAD_DIM = 64
NB_ROWS = 8
NB_COLS = 16
RG_BW = 128
CONV_W = 4
RG_C = 8.0

LANES = 128
SUBLANES = 8
MXU_DIM = 256
VMEM_BYTES = 64 * 1024 * 1024

HEADS_PER_GROUP = MXU_DIM // NA_HEAD_DIM
TOKEN_TILE = 512
FF_CHUNK = MXU_DIM


def _cparams(n_axes, vmem_mb):
    return pltpu.CompilerParams(
        dimension_semantics=("arbitrary",) * n_axes,
        vmem_limit_bytes=min(vmem_mb * 1024 * 1024, VMEM_BYTES - 8 * 1024 * 1024),
    )


def _resident(shape):
    nd = len(shape)
    return pl.BlockSpec(shape, lambda *_: (0,) * nd, pipeline_mode=pl.Buffered(1))


def _silu(x):
    return x * jax.nn.sigmoid(x)


def _gelu_tanh(x):
    c = np.sqrt(2.0 / np.pi).astype(np.float32)
    return 0.5 * x * (1.0 + jnp.tanh(c * (x + 0.044715 * (x * x * x))))


def _rms_mod(x, g, shift, scale):
    ms = jnp.mean(x * x, axis=-1, keepdims=True)
    y = (x * lax.rsqrt(ms + RMS_EPS)) * g
    return y * (1.0 + scale) + shift


def _swiglu(h, wgu_ref, wd_ref, act_ref):
    d_ff = wd_ref.shape[0]
    for c in range(d_ff // FF_CHUNK):
        lo = c * FF_CHUNK
        ug = jnp.dot(h, wgu_ref[:, lo:lo + FF_CHUNK], preferred_element_type=F32)
        uu = jnp.dot(h, wgu_ref[:, d_ff + lo:d_ff + lo + FF_CHUNK], preferred_element_type=F32)
        act_ref[:, lo:lo + FF_CHUNK] = (_silu(ug) * uu).astype(BF16)
    return jnp.dot(act_ref[...], wd_ref[...], preferred_element_type=F32)


WEIGHT_CHUNK_ROWS = 128


def _hbm():
    return pl.BlockSpec(memory_space=pl.ANY)


def _load_weights_bf16(jobs, stage_ref, sem):
    rows = WEIGHT_CHUNK_ROWS
    width = stage_ref.shape[-1]
    groups = []
    for src, dst in jobs:
        cols = dst.shape[1]
        chunks = list(range(dst.shape[0] // rows))
        per_slot = width // cols
        groups += [(src, dst, cols, chunks[k:k + per_slot]) for k in range(0, len(chunks), per_slot)]

    def copies(group, slot):
        src, _, cols, chunks = group
        return [pltpu.make_async_copy(src.at[pl.ds(i * rows, rows), :],
                                      stage_ref.at[slot, :, pl.ds(p * cols, cols)], sem.at[slot])
                for p, i in enumerate(chunks)]

    for cp in copies(groups[0], 0):
        cp.start()
    for k, group in enumerate(groups):
        slot = k % 2
        if k + 1 < len(groups):
            for cp in copies(groups[k + 1], 1 - slot):
                cp.start()
        for cp in copies(group, slot):
            cp.wait()
        _, dst, cols, chunks = group
        for p, i in enumerate(chunks):
            dst[i * rows:(i + 1) * rows, :] = stage_ref[slot, :, p * cols:(p + 1) * cols].astype(BF16)


def _weight_scratch(shapes):
    width = max(s[1] for s in shapes)
    return ([pltpu.VMEM(s, BF16) for s in shapes]
            + [pltpu.VMEM((2, WEIGHT_CHUNK_ROWS, width), F32), pltpu.SemaphoreType.DMA((2,))])


def _mods_kernel(c_ref, w_ref, b_ref, o_ref):
    h = _silu(c_ref[...]).astype(BF16)
    w = w_ref[0].astype(BF16)
    o_ref[0] = jnp.dot(h, w, preferred_element_type=F32) + b_ref[0]


def _mods(c, c_ctx, ada_w, ada_b):
    depth, d, n = ada_w.shape
    b = c.shape[0]
    rows = jnp.zeros((SUBLANES, d), F32).at[:b].set(c).at[b].set(c_ctx)
    tn = n // 4
    out = pl.pallas_call(
        _mods_kernel,
        grid=(depth, n // tn),
        in_specs=[
            pl.BlockSpec((SUBLANES, d), lambda i, j: (0, 0)),
            pl.BlockSpec((1, d, tn), lambda i, j: (i, 0, j)),
            pl.BlockSpec((1, 1, tn), lambda i, j: (i, 0, j)),
        ],
        out_specs=pl.BlockSpec((1, SUBLANES, tn), lambda i, j: (i, 0, j)),
        out_shape=jax.ShapeDtypeStruct((depth, SUBLANES, n), F32),
        compiler_params=_cparams(2, 40),
        name="mods",
    )(rows, ada_w, ada_b.reshape(depth, 1, n))
    return out.reshape(depth, SUBLANES, N_MOD, d)


class _TokenGrid:
    def __init__(self, n_batch, seq, n_ctx, with_ctx=True):
        self.tm = TOKEN_TILE
        self.n_batch = n_batch
        self.per_batch = seq // self.tm
        self.x_tiles = n_batch * self.per_batch
        self.c_tiles = (n_batch * n_ctx) // self.tm if with_ctx else 0
        self.n_x = n_batch * seq
        self.n_tok = self.n_x + (n_batch * n_ctx if with_ctx else 0)

    @property
    def grid(self):
        return (self.x_tiles + self.c_tiles,)

    def _x_tile(self, i):
        return jnp.minimum(i, self.x_tiles - 1)

    def _c_tile(self, i):
        return jnp.maximum(i - self.x_tiles, 0)

    def merged(self, width):
        return pl.BlockSpec((1, self.tm, width), lambda i: (0, i, 0))

    def x_only(self, width):
        return pl.BlockSpec((1, self.tm, width), lambda i: (0, self._x_tile(i), 0))

    def c_only(self, width):
        return pl.BlockSpec((1, self.tm, width), lambda i: (0, self._c_tile(i), 0))

    def x_slab(self, n_slabs):
        def index(i):
            t = self._x_tile(i)
            return (t // self.per_batch, 0, t % self.per_batch, 0)
        return pl.BlockSpec((1, n_slabs, self.tm, LANES), index)

    def c_slab(self, n_slabs):
        return pl.BlockSpec((1, n_slabs, self.tm, LANES), lambda i: (0, 0, self._c_tile(i), 0))

    def mod(self, d):
        return pl.BlockSpec((1, N_MOD, d), lambda i: (jnp.minimum(i // self.per_batch, self.n_batch), 0, 0))

    def is_x(self, step):
        return step < self.x_tiles


def _pre_kernel(*refs, splits, n_src, x_tiles, w_index):
    x_refs, mod_ref = refs[:n_src], refs[n_src]
    g_ref, wgu_hbm, wd_hbm, win_hbm = refs[n_src + 1:n_src + 5]
    pos = n_src + 5
    x_out = refs[pos]
    pos += 1
    out_refs = []
    for _, _, slab in splits:
        out_refs.append(refs[pos:pos + (2 if slab else 1)])
        pos += 2 if slab else 1
    act_ref, wgu_ref, wd_ref, win_ref, stage_ref, sem = refs[pos:]
    step = pl.program_id(0)
    layer, mixer = w_index
    is_x = step < x_tiles

    @pl.when(step == 0)
    def _():
        _load_weights_bf16([(wgu_hbm.at[layer, 0], wgu_ref), (wd_hbm.at[layer, 0], wd_ref),
                            (win_hbm.at[mixer], win_ref)], stage_ref, sem)

    x = x_refs[0][0] if n_src == 1 else jnp.where(is_x, x_refs[0][0], x_refs[1][0])
    mod = mod_ref[0]
    h = _rms_mod(x, g_ref[0:1, :], mod[0:1, :], mod[1:2, :]).astype(BF16)
    y = _swiglu(h, wgu_ref, wd_ref, act_ref)
    x = x + (0.5 * mod[2:3, :]) * y
    x_out[0] = x
    h = _rms_mod(x, g_ref[1:2, :], mod[3:4, :], mod[4:5, :]).astype(BF16)
    off = 0
    for o_refs, (width, scale, slab) in zip(out_refs, splits):
        u = jnp.dot(h, win_ref[:, off:off + width], preferred_element_type=F32)
        if scale != 1.0:
            u = u * scale
        off += width
        if not slab:
            o_refs[0][0] = u.astype(o_refs[0].dtype)
            continue

        def write(o_ref, u=u, width=width):
            for j in range(width // LANES):
                o_ref[0, j] = u[:, j * LANES:(j + 1) * LANES].astype(o_ref.dtype)

        pl.when(is_x)(functools.partial(write, o_refs[0]))
        pl.when(jnp.logical_not(is_x))(functools.partial(write, o_refs[1]))


def _pre(tg, xs, mod, g, wgu, wd, win, w_index, splits, dtypes, name):
    d = xs[0].shape[-1]
    n_src = len(xs)
    in_specs = [tg.merged(d)] if n_src == 1 else [tg.x_only(d), tg.c_only(d)]
    in_specs += [tg.mod(d), _resident(g.shape), _hbm(), _hbm(), _hbm()]
    out_shape = [jax.ShapeDtypeStruct((1, tg.n_tok, d), F32)]
    out_specs = [tg.merged(d)]
    for (w, _, is_slab), dt in zip(splits, dtypes):
        if is_slab:
            k = w // LANES
            out_shape += [jax.ShapeDtypeStruct((tg.n_batch, k, tg.n_x // tg.n_batch, LANES), dt),
                          jax.ShapeDtypeStruct((1, k, tg.n_tok - tg.n_x, LANES), dt)]
            out_specs += [tg.x_slab(k), tg.c_slab(k)]
        else:
            out_shape.append(jax.ShapeDtypeStruct((1, tg.n_tok, w), dt))
            out_specs.append(tg.merged(w))
    w_shapes = [wgu.shape[-2:], wd.shape[-2:], win.shape[-2:]]
    outs = pl.pallas_call(
        functools.partial(_pre_kernel, splits=splits, n_src=n_src, x_tiles=tg.x_tiles, w_index=w_index),
        grid=tg.grid,
        in_specs=in_specs,
        out_specs=out_specs,
        out_shape=out_shape,
        scratch_shapes=[pltpu.VMEM((tg.tm, wd.shape[-2]), BF16)] + _weight_scratch(w_shapes),
        compiler_params=_cparams(1, 56),
        name=name,
    )(*xs, mod, g, wgu, wd, win)
    result, pos = [outs[0]], 1
    for _, _, is_slab in splits:
        result.append(tuple(outs[pos:pos + 2]) if is_slab else outs[pos])
        pos += 2 if is_slab else 1
    return result


def _post_kernel(*refs, n_parts, final, gated_scan, x_tiles, w_index):
    x_ref, mod_ref, g_ref = refs[:3]
    pos = 3
    if gated_scan:
        part_refs = refs[pos:pos + 3]
        pos += 3
    else:
        part_refs = [refs[pos + 2 * j:pos + 2 * j + 2] for j in range(n_parts)]
        pos += 2 * n_parts
    wout_hbm, wgu_hbm, wd_hbm = refs[pos:pos + 3]
    pos += 3
    if final:
        gf_ref = refs[pos]
        pos += 1
    o_ref, act_ref = refs[pos:pos + 2]
    pos += 2
    if gated_scan:
        mix_ref = refs[pos]
        pos += 1
    wout_ref, wgu_ref, wd_ref, stage_ref, sem = refs[pos:]
    step = pl.program_id(0)
    layer, mixer = w_index

    @pl.when(step == 0)
    def _():
        _load_weights_bf16([(wout_hbm.at[mixer], wout_ref), (wgu_hbm.at[layer, 1], wgu_ref),
                            (wd_hbm.at[layer, 1], wd_ref)], stage_ref, sem)

    if gated_scan:
        gate_ref, hf_ref, hb_ref = part_refs
        for j in range(hf_ref.shape[1]):
            gate = gate_ref[0, :, j * LANES:(j + 1) * LANES].astype(F32)
            hsum = hf_ref[0, j].astype(F32) + hb_ref[0, j].astype(F32)
            mix_ref[:, j * LANES:(j + 1) * LANES] = (_gelu_tanh(gate) * hsum).astype(BF16)
        acc = jnp.dot(mix_ref[...], wout_ref[...], preferred_element_type=F32)
    else:
        is_x = step < x_tiles
        acc = None
        off = 0
        for px_ref, pc_ref in part_refs:
            y = jnp.where(is_x, px_ref[0], pc_ref[0])
            w = y.shape[-1]
            part = jnp.dot(y, wout_ref[off:off + w, :], preferred_element_type=F32)
            acc = part if acc is None else acc + part
            off += w
    mod = mod_ref[0]
    x = x_ref[0] + mod[5:6, :] * acc
    h = _rms_mod(x, g_ref[2:3, :], mod[6:7, :], mod[7:8, :]).astype(BF16)
    y = _swiglu(h, wgu_ref, wd_ref, act_ref)
    x = x + (0.5 * mod[8:9, :]) * y
    if final:
        ms = jnp.mean(x * x, axis=-1, keepdims=True)
        x = (x * lax.rsqrt(ms + RMS_EPS)) * gf_ref[...]
    o_ref[0] = x


def _post(tg, x, mod, g, parts, wout, wgu, wd, w_index, final_g, name, gated_scan=False):
    d = x.shape[-1]
    args = [x, mod, g]
    in_specs = [tg.merged(d), tg.mod(d), _resident(g.shape)]
    if gated_scan:
        gate, hf, hb = parts
        args += [gate, hf, hb]
        in_specs += [tg.merged(gate.shape[-1]), tg.x_slab(hf.shape[1]), tg.x_slab(hb.shape[1])]
    else:
        for px, pc in parts:
            args += [px, pc]
            in_specs += [tg.x_only(px.shape[-1]), tg.c_only(pc.shape[-1])]
    args += [wout, wgu, wd]
    in_specs += [_hbm(), _hbm(), _hbm()]
    if final_g is not None:
        args.append(final_g)
        in_specs.append(_resident(final_g.shape))
    scratch = [pltpu.VMEM((tg.tm, wd.shape[-2]), BF16)]
    if gated_scan:
        scratch.append(pltpu.VMEM((tg.tm, wout.shape[-2]), BF16))
    scratch += _weight_scratch([wout.shape[-2:], wgu.shape[-2:], wd.shape[-2:]])
    return pl.pallas_call(
        functools.partial(_post_kernel, n_parts=len(parts), final=final_g is not None,
                          gated_scan=gated_scan, x_tiles=tg.x_tiles, w_index=w_index),
        grid=tg.grid,
        in_specs=in_specs,
        out_specs=tg.merged(d),
        out_shape=jax.ShapeDtypeStruct((1, tg.n_tok, d), F32),
        scratch_shapes=scratch,
        compiler_params=_cparams(1, 56),
        name=name,
    )(*args)


POOL_PAD = SUBLANES * 2


def _pool_kernel(p_ref, w_ref, s_ref, o_ref, pad_ref):
    t = p_ref.shape[1]
    gw = w_ref.shape[-1]
    zeros = jnp.zeros((POOL_PAD, gw), F32)
    pad_ref[0:POOL_PAD, :] = zeros
    pad_ref[POOL_PAD + t:POOL_PAD + t + POOL_PAD, :] = zeros
    pos = lax.broadcasted_iota(jnp.int32, (t, gw), 0)
    for gi, win in enumerate(POOL_WINDOWS):
        half = win // 2
        u = p_ref[0, :, gi * gw:(gi + 1) * gw].astype(F32)
        pad_ref[POOL_PAD:POOL_PAD + t, :] = u
        wsum = None
        for k in range(-half, half):
            piece = pad_ref[POOL_PAD + k:POOL_PAD + k + t, :]
            wsum = piece if wsum is None else wsum + piece
        cnt = (jnp.minimum(pos + half, t) - jnp.maximum(pos - half, 0)).astype(F32)
        y = (wsum / cnt - u).astype(BF16)
        z = jnp.dot(y, w_ref[gi], preferred_element_type=F32)
        o_ref[0, :, gi * gw:(gi + 1) * gw] = (z * s_ref[:, gi * gw:(gi + 1) * gw]).astype(o_ref.dtype)


def _pool(p, pool_w, pool_scale, n, t, first_block, name):
    w = p.shape[-1]
    gw = pool_w.shape[-1]
    return pl.pallas_call(
        _pool_kernel,
        grid=(n,),
        in_specs=[
            pl.BlockSpec((1, t, w), lambda b: (0, first_block + b, 0)),
            _resident(pool_w.shape), _resident(pool_scale.shape),
        ],
        out_specs=pl.BlockSpec((1, t, w), lambda b: (0, b, 0)),
        out_shape=jax.ShapeDtypeStruct((1, n * t, w), BF16),
        scratch_shapes=[pltpu.VMEM((t + 2 * POOL_PAD, gw), F32)],
        compiler_params=_cparams(1, 48),
        name=name,
    )(p, pool_w, pool_scale)


def _head_block_mask(shape):
    r = lax.broadcasted_iota(jnp.int32, shape, 0) // NA_HEAD_DIM
    c = lax.broadcasted_iota(jnp.int32, shape, 1) // NA_HEAD_DIM
    return r == c


def _attend(q_blk, key_parts, val_parts, bias_parts):
    nq, g = q_blk.shape
    heads = g // NA_HEAD_DIM
    qbd = jnp.where(_head_block_mask((heads * nq, g)),
                    jnp.concatenate([q_blk] * heads, axis=0), jnp.zeros((), BF16))
    nt = (((1,), (1,)), ((), ()))
    scores = []
    for keys, bias in zip(key_parts, bias_parts):
        s = lax.dot_general(keys, qbd, nt, preferred_element_type=F32)
        scores.append(s if bias is None else s + bias)
    m = functools.reduce(jnp.maximum, [jnp.max(s, axis=0, keepdims=True) for s in scores])
    ps = [jnp.exp(s - m) for s in scores]
    l = functools.reduce(jnp.add, [jnp.sum(p, axis=0, keepdims=True) for p in ps])
    inv = 1.0 / l
    tn = (((0,), (0,)), ((), ()))
    r = None
    for p, vals in zip(ps, val_parts):
        part = lax.dot_general((p * inv).astype(BF16), vals, tn, preferred_element_type=F32)
        r = part if r is None else r + part
    lane_head = lax.broadcasted_iota(jnp.int32, (nq, g), 1) // NA_HEAD_DIM
    out = jnp.zeros((nq, g), F32)
    for h in range(heads):
        out = jnp.where(lane_head == h, r[h * nq:(h + 1) * nq, :], out)
    return out


NATTN_UNROLL = 8


def _nattn_kernel(q_ref, k_ref, v_ref, kc_ref, vc_ref, bias_ref, o_ref):
    s = q_ref.shape[1]
    rows = s // GRID_W
    kh = min(NB_ROWS, rows)
    kc = kc_ref[0]
    vc = vc_ref[0]

    def body(i, carry):
        r0 = jnp.clip(i - kh // 2, 0, rows - kh)
        q0 = pl.multiple_of(i * GRID_W, GRID_W)
        k0 = pl.multiple_of(r0 * GRID_W, GRID_W)
        q_blk = q_ref[0, pl.ds(q0, GRID_W), :]
        keys = k_ref[0, pl.ds(k0, kh * GRID_W), :]
        vals = v_ref[0, pl.ds(k0, kh * GRID_W), :]
        bias = bias_ref[0, pl.ds(r0 - i + (NB_ROWS - 1), kh), :, :]
        bias = bias.reshape(kh * GRID_W, bias.shape[-1])
        out = _attend(q_blk, [keys, kc], [vals, vc], [bias, None])
        o_ref[0, pl.ds(q0, GRID_W), :] = out.astype(o_ref.dtype)
        return carry

    lax.fori_loop(0, rows, body, 0, unroll=NATTN_UNROLL)


def _ctx_attn_kernel(q_ref, kc_ref, vc_ref, o_ref):
    t = q_ref.shape[1]
    kc = kc_ref[0]
    vc = vc_ref[0]
    for i in range(t // GRID_W):
        q_blk = q_ref[0, i * GRID_W:(i + 1) * GRID_W, :]
        out = _attend(q_blk, [kc], [vc], [None])
        o_ref[0, i * GRID_W:(i + 1) * GRID_W, :] = out.astype(o_ref.dtype)


def _bias_table(rpb):
    h = rpb.shape[0]
    kcol = np.arange(GRID_W)[:, None]
    qcol = np.arange(GRID_W)[None, :]
    qstart = np.clip(qcol - NB_COLS // 2, 0, GRID_W - NB_COLS)
    mask = (kcol >= qstart) & (kcol < qstart + NB_COLS)
    idx = np.clip(kcol - qcol + NB_COLS - 1, 0, 2 * NB_COLS - 2)
    onehot = (idx[None] == np.arange(rpb.shape[-1])[:, None, None]).astype(np.float32)
    t = jnp.einsum("hrj,jkq->hrkq", rpb.astype(F32), onehot, precision=lax.Precision.HIGHEST)
    t = jnp.where(mask[None, None], t, -jnp.inf)
    t = t.reshape(h // HEADS_PER_GROUP, HEADS_PER_GROUP, t.shape[1], GRID_W, GRID_W)
    t = jnp.transpose(t, (0, 2, 3, 1, 4))
    return t.reshape(h // HEADS_PER_GROUP, t.shape[1], GRID_W, HEADS_PER_GROUP * GRID_W)


def _nattn(q, k, v, rpb, b, s, n_ctx):
    w = q.shape[-1]
    g = HEADS_PER_GROUP * NA_HEAD_DIM
    bias = _bias_table(rpb)
    ctx0 = (b * s) // n_ctx
    lat = pl.BlockSpec((1, s, g), lambda i, j: (0, i, j))
    ctx = pl.BlockSpec((1, n_ctx, g), lambda i, j: (0, ctx0 + i, j))
    return pl.pallas_call(
        _nattn_kernel,
        grid=(b, w // g),
        in_specs=[lat, lat, lat, ctx, ctx,
                  pl.BlockSpec((1,) + bias.shape[1:], lambda i, j: (j, 0, 0, 0))],
        out_specs=lat,
        out_shape=jax.ShapeDtypeStruct((1, b * s, w), BF16),
        compiler_params=_cparams(2, 48),
        name="nattn",
    )(q, k, v, k, v, bias)


def _ctx_attn(q, k, v, b, s, n_ctx):
    w = q.shape[-1]
    g = HEADS_PER_GROUP * NA_HEAD_DIM
    ctx0 = (b * s) // n_ctx
    ctx = pl.BlockSpec((1, n_ctx, g), lambda i, j: (0, ctx0 + i, j))
    return pl.pallas_call(
        _ctx_attn_kernel,
        grid=(b, w // g),
        in_specs=[ctx, ctx, ctx],
        out_specs=pl.BlockSpec((1, n_ctx, g), lambda i, j: (0, i, j)),
        out_shape=jax.ShapeDtypeStruct((1, b * n_ctx, w), BF16),
        compiler_params=_cparams(2, 32),
        name="ctx_attn",
    )(q, k, v)


RG_CHUNK = 128
HALO = SUBLANES
SCAN_UNROLL = 8
COEF_UNROLL = 2


def _rglru_kernel(rxf_ref, rxf_prev_ref, rxf_next_ref, rxb_ref, rxb_prev_ref, rxb_next_ref, rc_ref,
                  cw_ref, cb_ref, gw_ref, gb_ref, lam_ref, hf_ref, hb_ref,
                  win_ref, af_ref, bf_ref, ab_ref, bb_ref, sf_ref, sb_ref, state_ref, *, n_ctx):
    nbatch, nblk, tc, _ = rxf_ref.shape
    ngroups = nblk // COEF_UNROLL
    ctx_chunks = n_ctx // tc
    x_chunks = pl.num_programs(0) - ctx_chunks
    c = pl.program_id(0)
    in_ctx = c < ctx_chunks

    @pl.when(c == 0)
    def _():
        state_ref[...] = jnp.zeros(state_ref.shape, F32)

    def fill_from_ctx(win, n, j):
        for b in range(nbatch):
            base = b * n_ctx + j * tc
            prev0 = pl.multiple_of(jnp.maximum(base - HALO, b * n_ctx), HALO)
            next0 = pl.multiple_of(jnp.minimum(base + tc, (b + 1) * n_ctx - HALO), HALO)
            win[b, 0:HALO, :] = jnp.where(j > 0, rc_ref[n, pl.ds(prev0, HALO), :], 0.0)
            win[b, HALO:HALO + tc, :] = rc_ref[n, pl.ds(pl.multiple_of(base, tc), tc), :]
            win[b, HALO + tc:2 * HALO + tc, :] = jnp.where(
                j < ctx_chunks - 1, rc_ref[n, pl.ds(next0, HALO), :], 0.0)

    def fill_from_x(win, n, j, cur_ref, prev_ref, next_ref):
        for b in range(nbatch):
            win[b, 0:HALO, :] = jnp.where(j > 0, prev_ref[b, n], 0.0)
            win[b, HALO:HALO + tc, :] = cur_ref[b, n]
            win[b, HALO + tc:2 * HALO + tc, :] = jnp.where(j < x_chunks - 1, next_ref[b, n], 0.0)

    def coefficients(d, j_ctx, j_x, x_refs, a_ref, b_ref):
        cols = slice(d * 2 * RG_BW, (d + 1) * 2 * RG_BW)

        def fill(n, win):
            pl.when(in_ctx)(lambda: fill_from_ctx(win, n, j_ctx))
            pl.when(jnp.logical_not(in_ctx))(lambda: fill_from_x(win, n, j_x, *x_refs))

        def body(g, p, win):
            n = g * COEF_UNROLL + p
            cw = cw_ref[n]
            hus = []
            for b in range(nbatch):
                y = cb_ref[n]
                for k in range(CONV_W):
                    lo = HALO + k - CONV_W // 2
                    y = y + win[b, lo:lo + tc, :] * cw[k:k + 1, :]
                hus.append(y)
            hu = jnp.concatenate(hus, axis=0)
            z = jnp.dot(hu.astype(BF16), gw_ref[n, :, cols], preferred_element_type=F32) + gb_ref[n, :, cols]
            t_r = jnp.tanh(z[:, :RG_BW])
            t_i = jnp.tanh(z[:, RG_BW:])
            lam = -lam_ref[n, d:d + 1, :]
            softplus = jnp.maximum(lam, 0.0) + jnp.log1p(jnp.exp(-jnp.abs(lam)))
            log_a = (t_r + 1.0) * ((-0.5 * RG_C) * softplus)
            a = jnp.exp(log_a)
            th = jnp.tanh(log_a)
            num = -2.0 * th
            root = jnp.where(num > 0.0, num * lax.rsqrt(num * (1.0 - th)), 0.0)
            coef = root * ((t_i + 1.0) * hu)
            for b in range(nbatch):
                rows = pl.ds(p * nbatch + b, tc, stride=SUBLANES)
                a_ref[g, rows, :] = a[b * tc:(b + 1) * tc]
                b_ref[g, rows, :] = coef[b * tc:(b + 1) * tc]

        def group(g, carry):
            for p in range(COEF_UNROLL):
                fill(g * COEF_UNROLL + p, win_ref.at[p])
            for p in range(COEF_UNROLL):
                body(g, p, win_ref.at[p])
            return carry

        lax.fori_loop(0, ngroups, group, 0)

    xf = c - ctx_chunks
    coefficients(0, c, xf, (rxf_ref, rxf_prev_ref, rxf_next_ref), af_ref, bf_ref)
    coefficients(1, ctx_chunks - 1 - c, x_chunks - 1 - xf, (rxb_ref, rxb_prev_ref, rxb_next_ref),
                 ab_ref, bb_ref)

    def scan_step(t, hs):
        out_f, out_b = [], []
        rows_f = pl.ds(pl.multiple_of(t * SUBLANES, SUBLANES), SUBLANES)
        rows_b = pl.ds(pl.multiple_of((tc - 1 - t) * SUBLANES, SUBLANES), SUBLANES)
        for g in range(ngroups):
            h = af_ref[g, rows_f, :] * hs[g] + bf_ref[g, rows_f, :]
            sf_ref[g, rows_f, :] = h
            out_f.append(h)
            h = ab_ref[g, rows_b, :] * hs[ngroups + g] + bb_ref[g, rows_b, :]
            sb_ref[g, rows_b, :] = h
            out_b.append(h)
        return tuple(out_f + out_b)

    hs = tuple(state_ref[d, g] for d in range(2) for g in range(ngroups))
    hs = lax.fori_loop(0, tc, scan_step, hs, unroll=SCAN_UNROLL)
    for d in range(2):
        for g in range(ngroups):
            state_ref[d, g] = hs[d * ngroups + g]

    @pl.when(jnp.logical_not(in_ctx))
    def _():
        def emit(g, carry):
            for p in range(COEF_UNROLL):
                for b in range(nbatch):
                    rows = pl.ds(p * nbatch + b, tc, stride=SUBLANES)
                    hf_ref[b, g * COEF_UNROLL + p] = sf_ref[g, rows, :].astype(hf_ref.dtype)
                    hb_ref[b, g * COEF_UNROLL + p] = sb_ref[g, rows, :].astype(hb_ref.dtype)
            return carry
        lax.fori_loop(0, ngroups, emit, 0)


def _rglru(rx, rc, n_ctx, conv_w, conv_b, gate_w, gate_b, lam):
    b, nb, s, _ = rx.shape
    tc = RG_CHUNK
    ctx_chunks = n_ctx // tc
    x_chunks = s // tc
    per_halo = tc // HALO
    gw = jnp.transpose(gate_w, (2, 3, 0, 1, 4)).reshape(nb, RG_BW, 4 * RG_BW).astype(BF16)
    gb = 0.5 * jnp.transpose(gate_b.reshape(2, 2, nb, RG_BW), (2, 0, 1, 3)).reshape(nb, 1, 4 * RG_BW)
    cw = 0.5 * jnp.transpose(conv_w.reshape(CONV_W, nb, RG_BW), (1, 0, 2))
    cb = 0.5 * conv_b.reshape(nb, 1, RG_BW)
    lam_s = jnp.transpose(lam.reshape(2, nb, RG_BW), (1, 0, 2))

    chunk_f = lambda c: jnp.maximum(c - ctx_chunks, 0)
    chunk_b = lambda c: jnp.minimum(x_chunks - 1 + ctx_chunks - c, x_chunks - 1)
    cur = lambda f: pl.BlockSpec((b, nb, tc, LANES), lambda c: (0, 0, f(c), 0))
    prev = lambda f: pl.BlockSpec((b, nb, HALO, LANES),
                                  lambda c: (0, 0, jnp.maximum(f(c) * per_halo - 1, 0), 0))
    nxt = lambda f: pl.BlockSpec((b, nb, HALO, LANES),
                                 lambda c: (0, 0, jnp.minimum((f(c) + 1) * per_halo, s // HALO - 1), 0))
    assert COEF_UNROLL * b == SUBLANES and nb % COEF_UNROLL == 0, (b, nb)
    ngroups = nb // COEF_UNROLL
    slab = pltpu.VMEM((ngroups, tc * SUBLANES, LANES), F32)
    window = pltpu.VMEM((COEF_UNROLL, b, tc + 2 * HALO, LANES), F32)
    state_shape = jax.ShapeDtypeStruct((b, nb, s, LANES), BF16)
    return pl.pallas_call(
        functools.partial(_rglru_kernel, n_ctx=n_ctx),
        grid=(ctx_chunks + x_chunks,),
        in_specs=[cur(chunk_f), prev(chunk_f), nxt(chunk_f), cur(chunk_b), prev(chunk_b), nxt(chunk_b),
                  _resident(rc.shape), _resident(cw.shape), _resident(cb.shape), _resident(gw.shape),
                  _resident(gb.shape), _resident(lam_s.shape)],
        out_specs=[cur(chunk_f), cur(chunk_b)],
        out_shape=[state_shape, state_shape],
        scratch_shapes=[window, slab, slab, slab, slab, slab, slab,
                        pltpu.VMEM((2, ngroups, SUBLANES, LANES), F32)],
        compiler_params=_cparams(1, 52),
        name="rglru",
    )(rx, rx, rx, rx, rx, rx, rc, cw, cb, gw, gb, lam_s)


def kernel(x, c, ctx, c_ctx, ada_w, ada_b, norm_g, ffn_w_gu, ffn_w_down, ab_w_in, ab_w_out, pool_w,
           pool_scale, na_rpb, rg_w_in, rg_conv_w, rg_conv_b, rg_gate_w, rg_gate_b, rg_lambda,
           rg_w_out, final_g):
    b, s, d = x.shape
    n_ctx = ctx.shape[1]
    depth = ada_w.shape[0]
    pool_width = pool_w.shape[1] * pool_w.shape[2]
    na_width = (ab_w_in.shape[-1] - pool_width) // 3
    d_rnn = rg_w_out.shape[1]
    na_scale = NA_HEAD_DIM ** -0.5

    mods = _mods(c, c_ctx, ada_w, ada_b)
    tg = _TokenGrid(b, s, n_ctx)
    tg_x = _TokenGrid(b, s, n_ctx, with_ctx=False)
    xs = [x.reshape(1, b * s, d), ctx.reshape(1, b * n_ctx, d)]

    for i in range(depth):
        last = i == depth - 1
        mod = mods[i, :b + 1]
        g = norm_g[i]
        j = i // 2
        fg = final_g.reshape(1, d) if last else None
        if i % 2 == 0:
            pw = pool_w[j].astype(BF16)
            ps = pool_scale[j].reshape(1, pool_width)
            splits = ((pool_width, 1.0, False), (na_width, na_scale, False), (na_width, 1.0, False),
                      (na_width, 1.0, False))
            dts = (BF16, BF16, BF16, BF16)
            xm, p, q, k, v = _pre(tg, xs, mod, g, ffn_w_gu, ffn_w_down, ab_w_in, (i, j), splits, dts,
                                  f"pre{i}")
            a_x = _pool(p, pw, ps, b, s, 0, f"pool{i}_x")
            b_x = _nattn(q, k, v, na_rpb[j], b, s, n_ctx)
            if last:
                raise NotImplementedError("a final pooling/attention layer is not needed at this depth")
            a_c = _pool(p, pw, ps, b, n_ctx, (b * s) // n_ctx, f"pool{i}_c")
            b_c = _ctx_attn(q, k, v, b, s, n_ctx)
            xm = _post(tg, xm, mod, g, [(a_x, a_c), (b_x, b_c)], ab_w_out, ffn_w_gu, ffn_w_down, (i, j),
                       fg, f"post{i}")
        else:
            splits = ((d_rnn, 1.0, False), (d_rnn, 1.0, True))
            dts = (BF16, F32)
            xm, gate, (rx, rc) = _pre(tg, xs, mod, g, ffn_w_gu, ffn_w_down, rg_w_in, (i, j), splits, dts,
                                      f"pre{i}")
            hf, hb = _rglru(rx, rc[0], n_ctx, rg_conv_w[j], rg_conv_b[j], rg_gate_w[j], rg_gate_b[j],
                            rg_lambda[j])
            if not last:
                raise NotImplementedError("context output of an RG-LRU layer is not needed at this depth")
            xm = _post(tg_x, xm, mod, g, (gate, hf, hb), rg_w_out, ffn_w_gu, ffn_w_down, (i, j), fg,
                       f"post{i}", gated_scan=True)
        xs = [xm]
    return xm.reshape(b, s, d)
```

```python
import functools

import numpy as np
import jax
import jax.numpy as jnp
from jax import lax
from jax.experimental import pallas as pl
from jax.experimental.pallas import tpu as pltpu

F32 = jnp.float32
BF16 = jnp.bfloat16

N_MOD = 9
RMS_EPS = 1e-6
GRID_W = 64
POOL_WINDOWS = (2, 4, 8, 16)
NA_HEAD_DIM = 64
NB_ROWS = 8
NB_COLS = 16
RG_BW = 128
CONV_W = 4
RG_C = 8.0
LOG2_E = float(np.log2(np.e))

LANES = 128
SUBLANES = 8
MXU_DIM = 256
VMEM_BYTES = 64 * 1024 * 1024

HEADS_PER_GROUP = MXU_DIM // NA_HEAD_DIM
TOKEN_TILE = 512
FF_CHUNK = MXU_DIM


def _cparams(n_axes, vmem_mb):
    return pltpu.CompilerParams(
        dimension_semantics=("arbitrary",) * n_axes,
        vmem_limit_bytes=min(vmem_mb * 1024 * 1024, VMEM_BYTES - 8 * 1024 * 1024),
    )


def _resident(shape):
    nd = len(shape)
    return pl.BlockSpec(shape, lambda *_: (0,) * nd, pipeline_mode=pl.Buffered(1))


def _silu(x):
    return x * jax.nn.sigmoid(x)


def _gelu_tanh(x):
    c = np.sqrt(2.0 / np.pi).astype(np.float32)
    return 0.5 * x * (1.0 + jnp.tanh(c * (x + 0.044715 * (x * x * x))))


def _rms_mod(x, g, shift, scale):
    ms = jnp.mean(x * x, axis=-1, keepdims=True)
    y = (x * lax.rsqrt(ms + RMS_EPS)) * g
    return y * (1.0 + scale) + shift


def _swiglu(h, wgu_ref, wd_ref, act_ref):
    d_ff = wd_ref.shape[0]
    for c in range(d_ff // FF_CHUNK):
        lo = c * FF_CHUNK
        ug = jnp.dot(h, wgu_ref[:, lo:lo + FF_CHUNK], preferred_element_type=F32)
        uu = jnp.dot(h, wgu_ref[:, d_ff + lo:d_ff + lo + FF_CHUNK], preferred_element_type=F32)
        act_ref[:, lo:lo + FF_CHUNK] = (_silu(ug) * uu).astype(BF16)
    return jnp.dot(act_ref[...], wd_ref[...], preferred_element_type=F32)


WEIGHT_CHUNK_ROWS = 128


def _hbm():
    return pl.BlockSpec(memory_space=pl.ANY)


def _load_weights_bf16(jobs, stage_ref, sem):
    rows = WEIGHT_CHUNK_ROWS
    width = stage_ref.shape[-1]
    groups = []
    for src, dst in jobs:
        cols = dst.shape[1]
        chunks = list(range(dst.shape[0] // rows))
        per_slot = width // cols
        groups += [(src, dst, cols, chunks[k:k + per_slot]) for k in range(0, len(chunks), per_slot)]

    def copies(group, slot):
        src, _, cols, chunks = group
        return [pltpu.make_async_copy(src.at[pl.ds(i * rows, rows), :],
                                      stage_ref.at[slot, :, pl.ds(p * cols, cols)], sem.at[slot])
                for p, i in enumerate(chunks)]

    for cp in copies(groups[0], 0):
        cp.start()
    for k, group in enumerate(groups):
        slot = k % 2
        if k + 1 < len(groups):
            for cp in copies(groups[k + 1], 1 - slot):
                cp.start()
        for cp in copies(group, slot):
            cp.wait()
        _, dst, cols, chunks = group
        for p, i in enumerate(chunks):
            dst[i * rows:(i + 1) * rows, :] = stage_ref[slot, :, p * cols:(p + 1) * cols].astype(BF16)


def _weight_scratch(shapes):
    width = max(s[1] for s in shapes)
    return ([pltpu.VMEM(s, BF16) for s in shapes]
            + [pltpu.VMEM((2, WEIGHT_CHUNK_ROWS, width), F32), pltpu.SemaphoreType.DMA((2,))])


def _mods_kernel(c_ref, w_ref, b_ref, o_ref):
    h = _silu(c_ref[...]).astype(BF16)
    w = w_ref[0].astype(BF16)
    o_ref[0] = jnp.dot(h, w, preferred_element_type=F32) + b_ref[0]


def _mods(c, c_ctx, ada_w, ada_b):
    depth, d, n = ada_w.shape
    b = c.shape[0]
    rows = jnp.zeros((SUBLANES, d), F32).at[:b].set(c).at[b].set(c_ctx)
    tn = n // 4
    out = pl.pallas_call(
        _mods_kernel,
        grid=(depth, n // tn),
        in_specs=[
            pl.BlockSpec((SUBLANES, d), lambda i, j: (0, 0)),
            pl.BlockSpec((1, d, tn), lambda i, j: (i, 0, j)),
            pl.BlockSpec((1, 1, tn), lambda i, j: (i, 0, j)),
        ],
        out_specs=pl.BlockSpec((1, SUBLANES, tn), lambda i, j: (i, 0, j)),
        out_shape=jax.ShapeDtypeStruct((depth, SUBLANES, n), F32),
        compiler_params=_cparams(2, 40),
        name="mods",
    )(rows, ada_w, ada_b.reshape(depth, 1, n))
    return out.reshape(depth, SUBLANES, N_MOD, d)


class _TokenGrid:
    def __init__(self, n_batch, seq, n_ctx, with_ctx=True):
        self.tm = TOKEN_TILE
        self.n_batch = n_batch
        self.per_batch = seq // self.tm
        self.x_tiles = n_batch * self.per_batch
        self.c_tiles = (n_batch * n_ctx) // self.tm if with_ctx else 0
        self.n_x = n_batch * seq
        self.n_tok = self.n_x + (n_batch * n_ctx if with_ctx else 0)

    @property
    def grid(self):
        return (self.x_tiles + self.c_tiles,)

    def _x_tile(self, i):
        return jnp.minimum(i, self.x_tiles - 1)

    def _c_tile(self, i):
        return jnp.maximum(i - self.x_tiles, 0)

    def merged(self, width):
        return pl.BlockSpec((1, self.tm, width), lambda i: (0, i, 0))

    def x_only(self, width):
        return pl.BlockSpec((1, self.tm, width), lambda i: (0, self._x_tile(i), 0))

    def c_only(self, width):
        return pl.BlockSpec((1, self.tm, width), lambda i: (0, self._c_tile(i), 0))

    def x_slab(self, n_slabs):
        def index(i):
            t = self._x_tile(i)
            return (t // self.per_batch, 0, t % self.per_batch, 0)
        return pl.BlockSpec((1, n_slabs, self.tm, LANES), index)

    def c_slab(self, n_slabs):
        return pl.BlockSpec((1, n_slabs, self.tm, LANES), lambda i: (0, 0, self._c_tile(i), 0))

    def mod(self, d):
        return pl.BlockSpec((1, N_MOD, d), lambda i: (jnp.minimum(i // self.per_batch, self.n_batch), 0, 0))

    def is_x(self, step):
        return step < self.x_tiles


def _pre_kernel(*refs, splits, n_src, x_tiles, w_index):
    x_refs, mod_ref = refs[:n_src], refs[n_src]
    g_ref, wgu_hbm, wd_hbm, win_hbm = refs[n_src + 1:n_src + 5]
    pos = n_src + 5
    x_out = refs[pos]
    pos += 1
    out_refs = []
    for _, _, slab in splits:
        out_refs.append(refs[pos:pos + (2 if slab else 1)])
        pos += 2 if slab else 1
    act_ref, wgu_ref, wd_ref, win_ref, stage_ref, sem = refs[pos:]
    step = pl.program_id(0)
    layer, mixer = w_index
    is_x = step < x_tiles

    @pl.when(step == 0)
    def _():
        _load_weights_bf16([(wgu_hbm.at[layer, 0], wgu_ref), (wd_hbm.at[layer, 0], wd_ref),
                            (win_hbm.at[mixer], win_ref)], stage_ref, sem)

    x = x_refs[0][0] if n_src == 1 else jnp.where(is_x, x_refs[0][0], x_refs[1][0])
    mod = mod_ref[0]
    h = _rms_mod(x, g_ref[0:1, :], mod[0:1, :], mod[1:2, :]).astype(BF16)
    y = _swiglu(h, wgu_ref, wd_ref, act_ref)
    x = x + (0.5 * mod[2:3, :]) * y
    x_out[0] = x
    h = _rms_mod(x, g_ref[1:2, :], mod[3:4, :], mod[4:5, :]).astype(BF16)
    off = 0
    for o_refs, (width, scale, slab) in zip(out_refs, splits):
        u = jnp.dot(h, win_ref[:, off:off + width], preferred_element_type=F32)
        if scale != 1.0:
            u = u * scale
        off += width
        if not slab:
            o_refs[0][0] = u.astype(o_refs[0].dtype)
            continue

        def write(o_ref, u=u, width=width):
            for j in range(width // LANES):
                o_ref[0, j] = u[:, j * LANES:(j + 1) * LANES].astype(o_ref.dtype)

        pl.when(is_x)(functools.partial(write, o_refs[0]))
        pl.when(jnp.logical_not(is_x))(functools.partial(write, o_refs[1]))


def _pre(tg, xs, mod, g, wgu, wd, win, w_index, splits, dtypes, name):
    d = xs[0].shape[-1]
    n_src = len(xs)
    in_specs = [tg.merged(d)] if n_src == 1 else [tg.x_only(d), tg.c_only(d)]
    in_specs += [tg.mod(d), _resident(g.shape), _hbm(), _hbm(), _hbm()]
    out_shape = [jax.ShapeDtypeStruct((1, tg.n_tok, d), F32)]
    out_specs = [tg.merged(d)]
    for (w, _, is_slab), dt in zip(splits, dtypes):
        if is_slab:
            k = w // LANES
            out_shape += [jax.ShapeDtypeStruct((tg.n_batch, k, tg.n_x // tg.n_batch, LANES), dt),
                          jax.ShapeDtypeStruct((1, k, tg.n_tok - tg.n_x, LANES), dt)]
            out_specs += [tg.x_slab(k), tg.c_slab(k)]
        else:
            out_shape.append(jax.ShapeDtypeStruct((1, tg.n_tok, w), dt))
            out_specs.append(tg.merged(w))
    w_shapes = [wgu.shape[-2:], wd.shape[-2:], win.shape[-2:]]
    outs = pl.pallas_call(
        functools.partial(_pre_kernel, splits=splits, n_src=n_src, x_tiles=tg.x_tiles, w_index=w_index),
        grid=tg.grid,
        in_specs=in_specs,
        out_specs=out_specs,
        out_shape=out_shape,
        scratch_shapes=[pltpu.VMEM((tg.tm, wd.shape[-2]), BF16)] + _weight_scratch(w_shapes),
        compiler_params=_cparams(1, 56),
        name=name,
    )(*xs, mod, g, wgu, wd, win)
    result, pos = [outs[0]], 1
    for _, _, is_slab in splits:
        result.append(tuple(outs[pos:pos + 2]) if is_slab else outs[pos])
        pos += 2 if is_slab else 1
    return result


def _post_kernel(*refs, n_parts, final, gated_scan, x_tiles, w_index):
    x_ref, mod_ref, g_ref = refs[:3]
    pos = 3
    if gated_scan:
        part_refs = refs[pos:pos + 3]
        pos += 3
    else:
        part_refs = [refs[pos + 2 * j:pos + 2 * j + 2] for j in range(n_parts)]
        pos += 2 * n_parts
    wout_hbm, wgu_hbm, wd_hbm = refs[pos:pos + 3]
    pos += 3
    if final:
        gf_ref = refs[pos]
        pos += 1
    o_ref, act_ref = refs[pos:pos + 2]
    pos += 2
    if gated_scan:
        mix_ref = refs[pos]
        pos += 1
    wout_ref, wgu_ref, wd_ref, stage_ref, sem = refs[pos:]
    step = pl.program_id(0)
    layer, mixer = w_index

    @pl.when(step == 0)
    def _():
        _load_weights_bf16([(wout_hbm.at[mixer], wout_ref), (wgu_hbm.at[layer, 1], wgu_ref),
                            (wd_hbm.at[layer, 1], wd_ref)], stage_ref, sem)

    if gated_scan:
        gate_ref, hf_ref, hb_ref = part_refs
        per_dot = MXU_DIM // LANES
        acc = None
        for j0 in range(0, hf_ref.shape[1], per_dot):
            for j in range(j0, j0 + per_dot):
                gate = gate_ref[0, :, j * LANES:(j + 1) * LANES].astype(F32)
                hsum = hf_ref[0, j].astype(F32) + hb_ref[0, j].astype(F32)
                mix_ref[:, j * LANES:(j + 1) * LANES] = (_gelu_tanh(gate) * hsum).astype(BF16)
            k = slice(j0 * LANES, (j0 + per_dot) * LANES)
            part = jnp.dot(mix_ref[:, k], wout_ref[k, :], preferred_element_type=F32)
            acc = part if acc is None else acc + part
    else:
        is_x = step < x_tiles
        acc = None
        off = 0
        for px_ref, pc_ref in part_refs:
            y = jnp.where(is_x, px_ref[0], pc_ref[0])
            w = y.shape[-1]
            part = jnp.dot(y, wout_ref[off:off + w, :], preferred_element_type=F32)
            acc = part if acc is None else acc + part
            off += w
    mod = mod_ref[0]
    x = x_ref[0] + mod[5:6, :] * acc
    h = _rms_mod(x, g_ref[2:3, :], mod[6:7, :], mod[7:8, :]).astype(BF16)
    y = _swiglu(h, wgu_ref, wd_ref, act_ref)
    x = x + (0.5 * mod[8:9, :]) * y
    if final:
        ms = jnp.mean(x * x, axis=-1, keepdims=True)
        x = (x * lax.rsqrt(ms + RMS_EPS)) * gf_ref[...]
    o_ref[0] = x


def _post(tg, x, mod, g, parts, wout, wgu, wd, w_index, final_g, name, gated_scan=False):
    d = x.shape[-1]
    args = [x, mod, g]
    in_specs = [tg.merged(d), tg.mod(d), _resident(g.shape)]
    if gated_scan:
        gate, hf, hb = parts
        args += [gate, hf, hb]
        in_specs += [tg.merged(gate.shape[-1]), tg.x_slab(hf.shape[1]), tg.x_slab(hb.shape[1])]
    else:
        for px, pc in parts:
            args += [px, pc]
            in_specs += [tg.x_only(px.shape[-1]), tg.c_only(pc.shape[-1])]
    args += [wout, wgu, wd]
    in_specs += [_hbm(), _hbm(), _hbm()]
    if final_g is not None:
        args.append(final_g)
        in_specs.append(_resident(final_g.shape))
    scratch = [pltpu.VMEM((tg.tm, wd.shape[-2]), BF16)]
    if gated_scan:
        scratch.append(pltpu.VMEM((tg.tm, wout.shape[-2]), BF16))
    scratch += _weight_scratch([wout.shape[-2:], wgu.shape[-2:], wd.shape[-2:]])
    return pl.pallas_call(
        functools.partial(_post_kernel, n_parts=len(parts), final=final_g is not None,
                          gated_scan=gated_scan, x_tiles=tg.x_tiles, w_index=w_index),
        grid=tg.grid,
        in_specs=in_specs,
        out_specs=tg.merged(d),
        out_shape=jax.ShapeDtypeStruct((1, tg.n_tok, d), F32),
        scratch_shapes=scratch,
        compiler_params=_cparams(1, 56),
        name=name,
    )(*args)


POOL_PAD = SUBLANES * 2


def _pool_kernel(p_ref, w_ref, s_ref, o_ref, pad_ref):
    t = p_ref.shape[1]
    gw = w_ref.shape[-1]
    zeros = jnp.zeros((POOL_PAD, gw), F32)
    pad_ref[0:POOL_PAD, :] = zeros
    pad_ref[POOL_PAD + t:POOL_PAD + t + POOL_PAD, :] = zeros
    pos = lax.broadcasted_iota(jnp.int32, (t, gw), 0)
    for gi, win in enumerate(POOL_WINDOWS):
        half = win // 2
        u = p_ref[0, :, gi * gw:(gi + 1) * gw].astype(F32)
        pad_ref[POOL_PAD:POOL_PAD + t, :] = u
        wsum = None
        for k in range(-half, half):
            piece = pad_ref[POOL_PAD + k:POOL_PAD + k + t, :]
            wsum = piece if wsum is None else wsum + piece
        cnt = (jnp.minimum(pos + half, t) - jnp.maximum(pos - half, 0)).astype(F32)
        y = (wsum / cnt - u).astype(BF16)
        z = jnp.dot(y, w_ref[gi], preferred_element_type=F32)
        o_ref[0, :, gi * gw:(gi + 1) * gw] = (z * s_ref[:, gi * gw:(gi + 1) * gw]).astype(o_ref.dtype)


def _pool(p, pool_w, pool_scale, n, t, first_block, name):
    w = p.shape[-1]
    gw = pool_w.shape[-1]
    return pl.pallas_call(
        _pool_kernel,
        grid=(n,),
        in_specs=[
            pl.BlockSpec((1, t, w), lambda b: (0, first_block + b, 0)),
            _resident(pool_w.shape), _resident(pool_scale.shape),
        ],
        out_specs=pl.BlockSpec((1, t, w), lambda b: (0, b, 0)),
        out_shape=jax.ShapeDtypeStruct((1, n * t, w), BF16),
        scratch_shapes=[pltpu.VMEM((t + 2 * POOL_PAD, gw), F32)],
        compiler_params=_cparams(1, 48),
        name=name,
    )(p, pool_w, pool_scale)


def _head_block_mask(shape):
    r = lax.broadcasted_iota(jnp.int32, shape, 0) // NA_HEAD_DIM
    c = lax.broadcasted_iota(jnp.int32, shape, 1) // NA_HEAD_DIM
    return r == c


def _attend(q_blk, key_parts, val_parts, bias_parts):
    nq, g = q_blk.shape
    heads = g // NA_HEAD_DIM
    qbd = jnp.where(_head_block_mask((heads * nq, g)),
                    jnp.concatenate([q_blk] * heads, axis=0), jnp.zeros((), BF16))
    nt = (((1,), (1,)), ((), ()))
    scores = []
    for keys, bias in zip(key_parts, bias_parts):
        s = lax.dot_general(keys, qbd, nt, preferred_element_type=F32)
        scores.append(s if bias is None else s + bias)
    m = functools.reduce(jnp.maximum, [jnp.max(s, axis=0, keepdims=True) for s in scores])
    ps = [jnp.exp2(s - m) for s in scores]
    l = functools.reduce(jnp.add, [jnp.sum(p, axis=0, keepdims=True) for p in ps])
    tn = (((0,), (0,)), ((), ()))
    r = None
    for p, vals in zip(ps, val_parts):
        part = lax.dot_general(p.astype(BF16), vals, tn, preferred_element_type=F32)
        r = part if r is None else r + part
    inv_t = jnp.transpose(jnp.broadcast_to(1.0 / l, (LANES, heads * nq)))
    inv_rows = jnp.concatenate([inv_t] * (g // LANES), axis=1)
    lane_head = lax.broadcasted_iota(jnp.int32, (nq, g), 1) // NA_HEAD_DIM
    out = jnp.zeros((nq, g), F32)
    for h in range(heads):
        rows = slice(h * nq, (h + 1) * nq)
        out = jnp.where(lane_head == h, r[rows, :] * inv_rows[rows, :], out)
    return out


NATTN_UNROLL = 8


def _nattn_kernel(q_ref, k_ref, v_ref, kc_ref, vc_ref, bias_ref, o_ref):
    s = q_ref.shape[1]
    rows = s // GRID_W
    kh = min(NB_ROWS, rows)
    kc = kc_ref[0]
    vc = vc_ref[0]

    def body(i, carry):
        r0 = jnp.clip(i - kh // 2, 0, rows - kh)
        q0 = pl.multiple_of(i * GRID_W, GRID_W)
        k0 = pl.multiple_of(r0 * GRID_W, GRID_W)
        q_blk = q_ref[0, pl.ds(q0, GRID_W), :]
        keys = k_ref[0, pl.ds(k0, kh * GRID_W), :]
        vals = v_ref[0, pl.ds(k0, kh * GRID_W), :]
        bias = bias_ref[0, pl.ds(r0 - i + (NB_ROWS - 1), kh), :, :]
        bias = bias.reshape(kh * GRID_W, bias.shape[-1])
        out = _attend(q_blk, [keys, kc], [vals, vc], [bias, None])
        o_ref[0, pl.ds(q0, GRID_W), :] = out.astype(o_ref.dtype)
        return carry

    lax.fori_loop(0, rows, body, 0, unroll=NATTN_UNROLL)


def _ctx_attn_kernel(q_ref, kc_ref, vc_ref, o_ref):
    t = q_ref.shape[1]
    kc = kc_ref[0]
    vc = vc_ref[0]
    for i in range(t // GRID_W):
        q_blk = q_ref[0, i * GRID_W:(i + 1) * GRID_W, :]
        out = _attend(q_blk, [kc], [vc], [None])
        o_ref[0, i * GRID_W:(i + 1) * GRID_W, :] = out.astype(o_ref.dtype)


def _bias_table(rpb):
    h = rpb.shape[0]
    kcol = np.arange(GRID_W)[:, None]
    qcol = np.arange(GRID_W)[None, :]
    qstart = np.clip(qcol - NB_COLS // 2, 0, GRID_W - NB_COLS)
    mask = (kcol >= qstart) & (kcol < qstart + NB_COLS)
    idx = np.clip(kcol - qcol + NB_COLS - 1, 0, 2 * NB_COLS - 2)
    onehot = (idx[None] == np.arange(rpb.shape[-1])[:, None, None]).astype(np.float32)
    t = jnp.einsum("hrj,jkq->hrkq", rpb.astype(F32), onehot, precision=lax.Precision.HIGHEST)
    t = jnp.where(mask[None, None], t * LOG2_E, -jnp.inf)
    t = t.reshape(h // HEADS_PER_GROUP, HEADS_PER_GROUP, t.shape[1], GRID_W, GRID_W)
    t = jnp.transpose(t, (0, 2, 3, 1, 4))
    return t.reshape(h // HEADS_PER_GROUP, t.shape[1], GRID_W, HEADS_PER_GROUP * GRID_W)


def _nattn(q, k, v, rpb, b, s, n_ctx):
    w = q.shape[-1]
    g = HEADS_PER_GROUP * NA_HEAD_DIM
    bias = _bias_table(rpb)
    ctx0 = (b * s) // n_ctx
    lat = pl.BlockSpec((1, s, g), lambda i, j: (0, i, j))
    ctx = pl.BlockSpec((1, n_ctx, g), lambda i, j: (0, ctx0 + i, j))
    return pl.pallas_call(
        _nattn_kernel,
        grid=(b, w // g),
        in_specs=[lat, lat, lat, ctx, ctx,
                  pl.BlockSpec((1,) + bias.shape[1:], lambda i, j: (j, 0, 0, 0))],
        out_specs=lat,
        out_shape=jax.ShapeDtypeStruct((1, b * s, w), BF16),
        compiler_params=_cparams(2, 48),
        name="nattn",
    )(q, k, v, k, v, bias)


def _ctx_attn(q, k, v, b, s, n_ctx):
    w = q.shape[-1]
    g = HEADS_PER_GROUP * NA_HEAD_DIM
    ctx0 = (b * s) // n_ctx
    ctx = pl.BlockSpec((1, n_ctx, g), lambda i, j: (0, ctx0 + i, j))
    return pl.pallas_call(
        _ctx_attn_kernel,
        grid=(b, w // g),
        in_specs=[ctx, ctx, ctx],
        out_specs=pl.BlockSpec((1, n_ctx, g), lambda i, j: (0, i, j)),
        out_shape=jax.ShapeDtypeStruct((1, b * n_ctx, w), BF16),
        compiler_params=_cparams(2, 32),
        name="ctx_attn",
    )(q, k, v)


RG_CHUNK = 128
HALO = SUBLANES
SCAN_UNROLL = 8
COEF_UNROLL = 2


def _rglru_kernel(rxf_ref, rxf_prev_ref, rxf_next_ref, rxb_ref, rxb_prev_ref, rxb_next_ref, rc_ref,
                  cw_ref, cb_ref, gw_ref, lam_ref, hf_ref, hb_ref,
                  win_ref, af_ref, bf_ref, ab_ref, bb_ref, sf_ref, sb_ref, state_ref, *, n_ctx):
    nbatch, nblk, tc, _ = rxf_ref.shape
    ngroups = nblk // COEF_UNROLL
    ctx_chunks = n_ctx // tc
    x_chunks = pl.num_programs(0) - ctx_chunks
    c = pl.program_id(0)
    in_ctx = c < ctx_chunks

    @pl.when(c == 0)
    def _():
        state_ref[...] = jnp.zeros(state_ref.shape, F32)

    lane = lax.broadcasted_iota(jnp.int32, (nbatch * tc, RG_BW), 1)
    bias_lanes = (lane < 2).astype(F32).astype(BF16)

    def fill_from_ctx(win, n, j):
        for b in range(nbatch):
            base = b * n_ctx + j * tc
            prev0 = pl.multiple_of(jnp.maximum(base - HALO, b * n_ctx), HALO)
            next0 = pl.multiple_of(jnp.minimum(base + tc, (b + 1) * n_ctx - HALO), HALO)
            win[b, 0:HALO, :] = jnp.where(j > 0, rc_ref[n, pl.ds(prev0, HALO), :], 0.0)
            win[b, HALO:HALO + tc, :] = rc_ref[n, pl.ds(pl.multiple_of(base, tc), tc), :]
            win[b, HALO + tc:2 * HALO + tc, :] = jnp.where(
                j < ctx_chunks - 1, rc_ref[n, pl.ds(next0, HALO), :], 0.0)

    def fill_from_x(win, n, j, cur_ref, prev_ref, next_ref):
        for b in range(nbatch):
            win[b, 0:HALO, :] = jnp.where(j > 0, prev_ref[b, n], 0.0)
            win[b, HALO:HALO + tc, :] = cur_ref[b, n]
            win[b, HALO + tc:2 * HALO + tc, :] = jnp.where(j < x_chunks - 1, next_ref[b, n], 0.0)

    def coefficients(d, j_ctx, j_x, x_refs, a_ref, b_ref):
        cols = slice(d * 2 * RG_BW, (d + 1) * 2 * RG_BW)

        def fill(n, win):
            pl.when(in_ctx)(lambda: fill_from_ctx(win, n, j_ctx))
            pl.when(jnp.logical_not(in_ctx))(lambda: fill_from_x(win, n, j_x, *x_refs))

        def body(g, p, win):
            n = g * COEF_UNROLL + p
            cw = cw_ref[n]
            hus = []
            for b in range(nbatch):
                y = cb_ref[n]
                for k in range(CONV_W):
                    lo = HALO + k - CONV_W // 2
                    y = y + win[b, lo:lo + tc, :] * cw[k:k + 1, :]
                hus.append(y)
            hu = jnp.concatenate(hus, axis=0)
            lhs = jnp.concatenate([hu.astype(BF16), bias_lanes], axis=1)
            z = jnp.dot(lhs, gw_ref[n, :, cols], preferred_element_type=F32)
            t_r = jnp.tanh(z[:, :RG_BW])
            t_i = jnp.tanh(z[:, RG_BW:])
            lam = -lam_ref[n, d:d + 1, :]
            softplus = jnp.maximum(lam, 0.0) + jnp.log1p(jnp.exp(-jnp.abs(lam)))
            log_a = (t_r + 1.0) * ((-0.5 * RG_C) * softplus)
            a = jnp.exp(log_a)
            th = jnp.tanh(log_a)
            num = -2.0 * th
            root = jnp.where(num > 0.0, num * lax.rsqrt(num * (1.0 - th)), 0.0)
            coef = root * ((t_i + 1.0) * hu)
            for b in range(nbatch):
                rows = pl.ds(p * nbatch + b, tc, stride=SUBLANES)
                a_ref[g, rows, :] = a[b * tc:(b + 1) * tc]
                b_ref[g, rows, :] = coef[b * tc:(b + 1) * tc]

        return fill, body

    xf = c - ctx_chunks
    stages = [coefficients(0, c, xf, (rxf_ref, rxf_prev_ref, rxf_next_ref), af_ref, bf_ref),
              coefficients(1, ctx_chunks - 1 - c, x_chunks - 1 - xf, (rxb_ref, rxb_prev_ref, rxb_next_ref),
                           ab_ref, bb_ref)]

    def group(g, carry):
        for d, (fill, _) in enumerate(stages):
            for p in range(COEF_UNROLL):
                fill(g * COEF_UNROLL + p, win_ref.at[d * COEF_UNROLL + p])
        for d, (_, body) in enumerate(stages):
            for p in range(COEF_UNROLL):
                body(g, p, win_ref.at[d * COEF_UNROLL + p])
        return carry

    lax.fori_loop(0, ngroups, group, 0)

    def scan_step(t, hs):
        out_f, out_b = [], []
        rows_f = pl.ds(pl.multiple_of(t * SUBLANES, SUBLANES), SUBLANES)
        rows_b = pl.ds(pl.multiple_of((tc - 1 - t) * SUBLANES, SUBLANES), SUBLANES)
        for g in range(ngroups):
            h = af_ref[g, rows_f, :] * hs[g] + bf_ref[g, rows_f, :]
            sf_ref[g, rows_f, :] = h
            out_f.append(h)
            h = ab_ref[g, rows_b, :] * hs[ngroups + g] + bb_ref[g, rows_b, :]
            sb_ref[g, rows_b, :] = h
            out_b.append(h)
        return tuple(out_f + out_b)

    hs = tuple(state_ref[d, g] for d in range(2) for g in range(ngroups))
    hs = lax.fori_loop(0, tc, scan_step, hs, unroll=SCAN_UNROLL)
    for d in range(2):
        for g in range(ngroups):
            state_ref[d, g] = hs[d * ngroups + g]

    @pl.when(jnp.logical_not(in_ctx))
    def _():
        def emit(g, carry):
            for p in range(COEF_UNROLL):
                for b in range(nbatch):
                    rows = pl.ds(p * nbatch + b, tc, stride=SUBLANES)
                    hf_ref[b, g * COEF_UNROLL + p] = sf_ref[g, rows, :].astype(hf_ref.dtype)
                    hb_ref[b, g * COEF_UNROLL + p] = sb_ref[g, rows, :].astype(hb_ref.dtype)
            return carry
        lax.fori_loop(0, ngroups, emit, 0)


def _rglru(rx, rc, n_ctx, conv_w, conv_b, gate_w, gate_b, lam):
    b, nb, s, _ = rx.shape
    tc = RG_CHUNK
    ctx_chunks = n_ctx // tc
    x_chunks = s // tc
    per_halo = tc // HALO
    gw = jnp.transpose(gate_w, (2, 3, 0, 1, 4)).reshape(nb, RG_BW, 4 * RG_BW).astype(BF16)
    gb = 0.5 * jnp.transpose(gate_b.reshape(2, 2, nb, RG_BW), (2, 0, 1, 3)).reshape(nb, 1, 4 * RG_BW)
    gb_hi = gb.astype(BF16)
    gb_lo = (gb - gb_hi.astype(F32)).astype(BF16)
    pad = jnp.zeros((nb, MXU_DIM - RG_BW - 2, 4 * RG_BW), BF16)
    gw = jnp.concatenate([gw, gb_hi, gb_lo, pad], axis=1)
    cw = 0.5 * jnp.transpose(conv_w.reshape(CONV_W, nb, RG_BW), (1, 0, 2))
    cb = 0.5 * conv_b.reshape(nb, 1, RG_BW)
    lam_s = jnp.transpose(lam.reshape(2, nb, RG_BW), (1, 0, 2))

    chunk_f = lambda c: jnp.maximum(c - ctx_chunks, 0)
    chunk_b = lambda c: jnp.minimum(x_chunks - 1 + ctx_chunks - c, x_chunks - 1)
    cur = lambda f: pl.BlockSpec((b, nb, tc, LANES), lambda c: (0, 0, f(c), 0))
    prev = lambda f: pl.BlockSpec((b, nb, HALO, LANES),
                                  lambda c: (0, 0, jnp.maximum(f(c) * per_halo - 1, 0), 0))
    nxt = lambda f: pl.BlockSpec((b, nb, HALO, LANES),
                                 lambda c: (0, 0, jnp.minimum((f(c) + 1) * per_halo, s // HALO - 1), 0))
    assert COEF_UNROLL * b == SUBLANES and nb % COEF_UNROLL == 0, (b, nb)
    ngroups = nb // COEF_UNROLL
    slab = pltpu.VMEM((ngroups, tc * SUBLANES, LANES), F32)
    window = pltpu.VMEM((2 * COEF_UNROLL, b, tc + 2 * HALO, LANES), F32)
    state_shape = jax.ShapeDtypeStruct((b, nb, s, LANES), BF16)
    return pl.pallas_call(
        functools.partial(_rglru_kernel, n_ctx=n_ctx),
        grid=(ctx_chunks + x_chunks,),
        in_specs=[cur(chunk_f), prev(chunk_f), nxt(chunk_f), cur(chunk_b), prev(chunk_b), nxt(chunk_b),
                  _resident(rc.shape), _resident(cw.shape), _resident(cb.shape), _resident(gw.shape),
                  _resident(lam_s.shape)],
        out_specs=[cur(chunk_f), cur(chunk_b)],
        out_shape=[state_shape, state_shape],
        scratch_shapes=[window, slab, slab, slab, slab, slab, slab,
                        pltpu.VMEM((2, ngroups, SUBLANES, LANES), F32)],
        compiler_params=_cparams(1, 52),
        name="rglru",
    )(rx, rx, rx, rx, rx, rx, rc, cw, cb, gw, lam_s)


def kernel(x, c, ctx, c_ctx, ada_w, ada_b, norm_g, ffn_w_gu, ffn_w_down, ab_w_in, ab_w_out, pool_w,
           pool_scale, na_rpb, rg_w_in, rg_conv_w, rg_conv_b, rg_gate_w, rg_gate_b, rg_lambda,
           rg_w_out, final_g):
    b, s, d = x.shape
    n_ctx = ctx.shape[1]
    depth = ada_w.shape[0]
    pool_width = pool_w.shape[1] * pool_w.shape[2]
    na_width = (ab_w_in.shape[-1] - pool_width) // 3
    d_rnn = rg_w_out.shape[1]
    na_scale = NA_HEAD_DIM ** -0.5 * LOG2_E

    mods = _mods(c, c_ctx, ada_w, ada_b)
    tg = _TokenGrid(b, s, n_ctx)
    tg_x = _TokenGrid(b, s, n_ctx, with_ctx=False)
    xs = [x.reshape(1, b * s, d), ctx.reshape(1, b * n_ctx, d)]

    for i in range(depth):
        last = i == depth - 1
        mod = mods[i, :b + 1]
        g = norm_g[i]
        j = i // 2
        fg = final_g.reshape(1, d) if last else None
        if i % 2 == 0:
            pw = pool_w[j].astype(BF16)
            ps = pool_scale[j].reshape(1, pool_width)
            splits = ((pool_width, 1.0, False), (na_width, na_scale, False), (na_width, 1.0, False),
                      (na_width, 1.0, False))
            dts = (BF16, BF16, BF16, BF16)
            xm, p, q, k, v = _pre(tg, xs, mod, g, ffn_w_gu, ffn_w_down, ab_w_in, (i, j), splits, dts,
                                  f"pre{i}")
            a_x = _pool(p, pw, ps, b, s, 0, f"pool{i}_x")
            b_x = _nattn(q, k, v, na_rpb[j], b, s, n_ctx)
            if last:
                raise NotImplementedError("a final pooling/attention layer is not needed at this depth")
            a_c = _pool(p, pw, ps, b, n_ctx, (b * s) // n_ctx, f"pool{i}_c")
            b_c = _ctx_attn(q, k, v, b, s, n_ctx)
            xm = _post(tg, xm, mod, g, [(a_x, a_c), (b_x, b_c)], ab_w_out, ffn_w_gu, ffn_w_down, (i, j),
                       fg, f"post{i}")
        else:
            splits = ((d_rnn, 1.0, False), (d_rnn, 1.0, True))
            dts = (BF16, F32)
            xm, gate, (rx, rc) = _pre(tg, xs, mod, g, ffn_w_gu, ffn_w_down, rg_w_in, (i, j), splits, dts,
                                      f"pre{i}")
            hf, hb = _rglru(rx, rc[0], n_ctx, rg_conv_w[j], rg_conv_b[j], rg_gate_w[j], rg_gate_b[j],
                            rg_lambda[j])
            if not last:
                raise NotImplementedError("context output of an RG-LRU layer is not needed at this depth")
            xm = _post(tg_x, xm, mod, g, (gate, hf, hb), rg_w_out, ffn_w_gu, ffn_w_down, (i, j), fg,
                       f"post{i}", gated_scan=True)
        xs = [xm]
    return xm.reshape(b, s, d)
```

```python
import functools

import numpy as np
import jax
import jax.numpy as jnp
from jax import lax
from jax.experimental import pallas as pl
from jax.experimental.pallas import tpu as pltpu

F32 = jnp.float32
BF16 = jnp.bfloat16

N_MOD = 9
RMS_EPS = 1e-6
GRID_W = 64
POOL_WINDOWS = (2, 4, 8, 16)
NA_HEAD_DIM = 64
NB_ROWS = 8
NB_COLS = 16
RG_BW = 128
CONV_W = 4
RG_C = 8.0
LOG2_E = float(np.log2(np.e))

LANES = 128
SUBLANES = 8
MXU_DIM = 256
VMEM_BYTES = 64 * 1024 * 1024

HEADS_PER_GROUP = MXU_DIM // NA_HEAD_DIM
TOKEN_TILE = 512
FF_CHUNK = MXU_DIM


def _cparams(n_axes, vmem_mb):
    return pltpu.CompilerParams(
        dimension_semantics=("arbitrary",) * n_axes,
        vmem_limit_bytes=min(vmem_mb * 1024 * 1024, VMEM_BYTES - 8 * 1024 * 1024),
    )


def _resident(shape):
    nd = len(shape)
    return pl.BlockSpec(shape, lambda *_: (0,) * nd, pipeline_mode=pl.Buffered(1))


def _silu(x):
    return x * jax.nn.sigmoid(x)


def _gelu_tanh(x):
    c = np.sqrt(2.0 / np.pi).astype(np.float32)
    return 0.5 * x * (1.0 + jnp.tanh(c * (x + 0.044715 * (x * x * x))))


def _rms_mod(x, g, shift, scale):
    ms = jnp.mean(x * x, axis=-1, keepdims=True)
    y = (x * lax.rsqrt(ms + RMS_EPS)) * g
    return y * (1.0 + scale) + shift


def _swiglu(h, wgu_ref, wd_ref, act_ref):
    d_ff = wd_ref.shape[0]
    for lo in range(0, d_ff, FF_CHUNK):
        hi = min(lo + FF_CHUNK, d_ff)
        ug = jnp.dot(h, wgu_ref[:, lo:hi], preferred_element_type=F32)
        uu = jnp.dot(h, wgu_ref[:, d_ff + lo:d_ff + hi], preferred_element_type=F32)
        act_ref[:, lo:hi] = (_silu(ug) * uu).astype(BF16)
    return jnp.dot(act_ref[...], wd_ref[...], preferred_element_type=F32)


WEIGHT_CHUNK_ROWS = 128


def _hbm():
    return pl.BlockSpec(memory_space=pl.ANY)


def _load_weights_bf16(jobs, stage_ref, sem):
    rows = WEIGHT_CHUNK_ROWS
    width = stage_ref.shape[-1]
    groups = []
    for src, dst in jobs:
        cols = dst.shape[1]
        chunks = list(range(dst.shape[0] // rows))
        per_slot = width // cols
        groups += [(src, dst, cols, chunks[k:k + per_slot]) for k in range(0, len(chunks), per_slot)]

    def copies(group, slot):
        src, _, cols, chunks = group
        return [pltpu.make_async_copy(src.at[pl.ds(i * rows, rows), :],
                                      stage_ref.at[slot, :, pl.ds(p * cols, cols)], sem.at[slot])
                for p, i in enumerate(chunks)]

    for cp in copies(groups[0], 0):
        cp.start()
    for k, group in enumerate(groups):
        slot = k % 2
        if k + 1 < len(groups):
            for cp in copies(groups[k + 1], 1 - slot):
                cp.start()
        for cp in copies(group, slot):
            cp.wait()
        _, dst, cols, chunks = group
        for p, i in enumerate(chunks):
            dst[i * rows:(i + 1) * rows, :] = stage_ref[slot, :, p * cols:(p + 1) * cols].astype(BF16)


def _weight_scratch(shapes):
    width = max(s[1] for s in shapes)
    return ([pltpu.VMEM(s, BF16) for s in shapes]
            + [pltpu.VMEM((2, WEIGHT_CHUNK_ROWS, width), F32), pltpu.SemaphoreType.DMA((2,))])


def _mods_kernel(c_ref, w_ref, b_ref, o_ref):
    h = _silu(c_ref[...]).astype(BF16)
    w = w_ref[0].astype(BF16)
    o_ref[0] = jnp.dot(h, w, preferred_element_type=F32) + b_ref[0]


def _mods(c, c_ctx, ada_w, ada_b):
    depth, d, n = ada_w.shape
    b = c.shape[0]
    rows = jnp.zeros((SUBLANES, d), F32).at[:b].set(c).at[b].set(c_ctx)
    tn = n // 4
    out = pl.pallas_call(
        _mods_kernel,
        grid=(depth, n // tn),
        in_specs=[
            pl.BlockSpec((SUBLANES, d), lambda i, j: (0, 0)),
            pl.BlockSpec((1, d, tn), lambda i, j: (i, 0, j)),
            pl.BlockSpec((1, 1, tn), lambda i, j: (i, 0, j)),
        ],
        out_specs=pl.BlockSpec((1, SUBLANES, tn), lambda i, j: (i, 0, j)),
        out_shape=jax.ShapeDtypeStruct((depth, SUBLANES, n), F32),
        compiler_params=_cparams(2, 40),
        name="mods",
    )(rows, ada_w, ada_b.reshape(depth, 1, n))
    return out.reshape(depth, SUBLANES, N_MOD, d)


class _TokenGrid:
    def __init__(self, n_batch, seq, n_ctx, with_ctx=True):
        self.tm = TOKEN_TILE
        self.n_batch = n_batch
        self.per_batch = seq // self.tm
        self.x_tiles = n_batch * self.per_batch
        self.c_tiles = (n_batch * n_ctx) // self.tm if with_ctx else 0
        self.n_x = n_batch * seq
        self.n_tok = self.n_x + (n_batch * n_ctx if with_ctx else 0)

    @property
    def grid(self):
        return (self.x_tiles + self.c_tiles,)

    def _x_tile(self, i):
        return jnp.minimum(i, self.x_tiles - 1)

    def _c_tile(self, i):
        return jnp.maximum(i - self.x_tiles, 0)

    def merged(self, width):
        return pl.BlockSpec((1, self.tm, width), lambda i: (0, i, 0))

    def x_only(self, width):
        return pl.BlockSpec((1, self.tm, width), lambda i: (0, self._x_tile(i), 0))

    def c_only(self, width):
        return pl.BlockSpec((1, self.tm, width), lambda i: (0, self._c_tile(i), 0))

    def x_slab(self, n_slabs):
        def index(i):
            t = self._x_tile(i)
            return (t // self.per_batch, 0, t % self.per_batch, 0)
        return pl.BlockSpec((1, n_slabs, self.tm, LANES), index)

    def c_slab(self, n_slabs):
        return pl.BlockSpec((1, n_slabs, self.tm, LANES), lambda i: (0, 0, self._c_tile(i), 0))

    def mod(self, d):
        return pl.BlockSpec((1, N_MOD, d), lambda i: (jnp.minimum(i // self.per_batch, self.n_batch), 0, 0))

    def is_x(self, step):
        return step < self.x_tiles


def _pre_kernel(*refs, splits, n_src, x_tiles, w_index):
    x_refs, mod_ref = refs[:n_src], refs[n_src]
    g_ref, wgu_hbm, wd_hbm, win_hbm = refs[n_src + 1:n_src + 5]
    pos = n_src + 5
    x_out = refs[pos]
    pos += 1
    out_refs = []
    for _, _, slab in splits:
        out_refs.append(refs[pos:pos + (2 if slab else 1)])
        pos += 2 if slab else 1
    act_ref, wgu_ref, wd_ref, win_ref, stage_ref, sem = refs[pos:]
    step = pl.program_id(0)
    layer, mixer = w_index
    is_x = step < x_tiles

    @pl.when(step == 0)
    def _():
        _load_weights_bf16([(wgu_hbm.at[layer, 0], wgu_ref), (wd_hbm.at[layer, 0], wd_ref),
                            (win_hbm.at[mixer], win_ref)], stage_ref, sem)

    x = x_refs[0][0] if n_src == 1 else jnp.where(is_x, x_refs[0][0], x_refs[1][0])
    mod = mod_ref[0]
    h = _rms_mod(x, g_ref[0:1, :], mod[0:1, :], mod[1:2, :]).astype(BF16)
    y = _swiglu(h, wgu_ref, wd_ref, act_ref)
    x = x + (0.5 * mod[2:3, :]) * y
    x_out[0] = x
    h = _rms_mod(x, g_ref[1:2, :], mod[3:4, :], mod[4:5, :]).astype(BF16)
    off = 0
    for o_refs, (width, scale, slab) in zip(out_refs, splits):
        u = jnp.dot(h, win_ref[:, off:off + width], preferred_element_type=F32)
        if scale != 1.0:
            u = u * scale
        off += width
        if not slab:
            o_refs[0][0] = u.astype(o_refs[0].dtype)
            continue

        def write(o_ref, u=u, width=width):
            for j in range(width // LANES):
                o_ref[0, j] = u[:, j * LANES:(j + 1) * LANES].astype(o_ref.dtype)

        pl.when(is_x)(functools.partial(write, o_refs[0]))
        pl.when(jnp.logical_not(is_x))(functools.partial(write, o_refs[1]))


def _pre(tg, xs, mod, g, wgu, wd, win, w_index, splits, dtypes, name):
    d = xs[0].shape[-1]
    n_src = len(xs)
    in_specs = [tg.merged(d)] if n_src == 1 else [tg.x_only(d), tg.c_only(d)]
    in_specs += [tg.mod(d), _resident(g.shape), _hbm(), _hbm(), _hbm()]
    out_shape = [jax.ShapeDtypeStruct((1, tg.n_tok, d), F32)]
    out_specs = [tg.merged(d)]
    for (w, _, is_slab), dt in zip(splits, dtypes):
        if is_slab:
            k = w // LANES
            out_shape += [jax.ShapeDtypeStruct((tg.n_batch, k, tg.n_x // tg.n_batch, LANES), dt),
                          jax.ShapeDtypeStruct((1, k, tg.n_tok - tg.n_x, LANES), dt)]
            out_specs += [tg.x_slab(k), tg.c_slab(k)]
        else:
            out_shape.append(jax.ShapeDtypeStruct((1, tg.n_tok, w), dt))
            out_specs.append(tg.merged(w))
    w_shapes = [wgu.shape[-2:], wd.shape[-2:], win.shape[-2:]]
    outs = pl.pallas_call(
        functools.partial(_pre_kernel, splits=splits, n_src=n_src, x_tiles=tg.x_tiles, w_index=w_index),
        grid=tg.grid,
        in_specs=in_specs,
        out_specs=out_specs,
        out_shape=out_shape,
        scratch_shapes=[pltpu.VMEM((tg.tm, wd.shape[-2]), BF16)] + _weight_scratch(w_shapes),
        compiler_params=_cparams(1, 56),
        name=name,
    )(*xs, mod, g, wgu, wd, win)
    result, pos = [outs[0]], 1
    for _, _, is_slab in splits:
        result.append(tuple(outs[pos:pos + 2]) if is_slab else outs[pos])
        pos += 2 if is_slab else 1
    return result


def _post_kernel(*refs, n_parts, final, gated_scan, x_tiles, w_index):
    x_ref, mod_ref, g_ref = refs[:3]
    pos = 3
    if gated_scan:
        part_refs = refs[pos:pos + 3]
        pos += 3
    else:
        part_refs = [refs[pos + 2 * j:pos + 2 * j + 2] for j in range(n_parts)]
        pos += 2 * n_parts
    wout_hbm, wgu_hbm, wd_hbm = refs[pos:pos + 3]
    pos += 3
    if final:
        gf_ref = refs[pos]
        pos += 1
    o_ref, act_ref = refs[pos:pos + 2]
    pos += 2
    if gated_scan:
        mix_ref = refs[pos]
        pos += 1
    wout_ref, wgu_ref, wd_ref, stage_ref, sem = refs[pos:]
    step = pl.program_id(0)
    layer, mixer = w_index

    @pl.when(step == 0)
    def _():
        _load_weights_bf16([(wout_hbm.at[mixer], wout_ref), (wgu_hbm.at[layer, 1], wgu_ref),
                            (wd_hbm.at[layer, 1], wd_ref)], stage_ref, sem)

    if gated_scan:
        gate_ref, hf_ref, hb_ref = part_refs
        per_dot = MXU_DIM // LANES
        acc = None
        for j0 in range(0, hf_ref.shape[1], per_dot):
            for j in range(j0, j0 + per_dot):
                gate = gate_ref[0, :, j * LANES:(j + 1) * LANES].astype(F32)
                hsum = hf_ref[0, j].astype(F32) + hb_ref[0, j].astype(F32)
                mix_ref[:, j * LANES:(j + 1) * LANES] = (_gelu_tanh(gate) * hsum).astype(BF16)
            k = slice(j0 * LANES, (j0 + per_dot) * LANES)
            part = jnp.dot(mix_ref[:, k], wout_ref[k, :], preferred_element_type=F32)
            acc = part if acc is None else acc + part
    else:
        is_x = step < x_tiles
        acc = None
        off = 0
        for px_ref, pc_ref in part_refs:
            y = jnp.where(is_x, px_ref[0], pc_ref[0])
            w = y.shape[-1]
            part = jnp.dot(y, wout_ref[off:off + w, :], preferred_element_type=F32)
            acc = part if acc is None else acc + part
            off += w
    mod = mod_ref[0]
    x = x_ref[0] + mod[5:6, :] * acc
    h = _rms_mod(x, g_ref[2:3, :], mod[6:7, :], mod[7:8, :]).astype(BF16)
    y = _swiglu(h, wgu_ref, wd_ref, act_ref)
    x = x + (0.5 * mod[8:9, :]) * y
    if final:
        ms = jnp.mean(x * x, axis=-1, keepdims=True)
        x = (x * lax.rsqrt(ms + RMS_EPS)) * gf_ref[...]
    o_ref[0] = x


def _post(tg, x, mod, g, parts, wout, wgu, wd, w_index, final_g, name, gated_scan=False):
    d = x.shape[-1]
    args = [x, mod, g]
    in_specs = [tg.merged(d), tg.mod(d), _resident(g.shape)]
    if gated_scan:
        gate, hf, hb = parts
        args += [gate, hf, hb]
        in_specs += [tg.merged(gate.shape[-1]), tg.x_slab(hf.shape[1]), tg.x_slab(hb.shape[1])]
    else:
        for px, pc in parts:
            args += [px, pc]
            in_specs += [tg.x_only(px.shape[-1]), tg.c_only(pc.shape[-1])]
    args += [wout, wgu, wd]
    in_specs += [_hbm(), _hbm(), _hbm()]
    if final_g is not None:
        args.append(final_g)
        in_specs.append(_resident(final_g.shape))
    scratch = [pltpu.VMEM((tg.tm, wd.shape[-2]), BF16)]
    if gated_scan:
        scratch.append(pltpu.VMEM((tg.tm, wout.shape[-2]), BF16))
    scratch += _weight_scratch([wout.shape[-2:], wgu.shape[-2:], wd.shape[-2:]])
    return pl.pallas_call(
        functools.partial(_post_kernel, n_parts=len(parts), final=final_g is not None,
                          gated_scan=gated_scan, x_tiles=tg.x_tiles, w_index=w_index),
        grid=tg.grid,
        in_specs=in_specs,
        out_specs=tg.merged(d),
        out_shape=jax.ShapeDtypeStruct((1, tg.n_tok, d), F32),
        scratch_shapes=scratch,
        compiler_params=_cparams(1, 56),
        name=name,
    )(*args)


POOL_PAD = SUBLANES * 2


def _pool_kernel(p_ref, w_ref, s_ref, o_ref, pad_ref):
    t = p_ref.shape[1]
    gw = w_ref.shape[-1]
    zeros = jnp.zeros((POOL_PAD, gw), F32)
    pad_ref[0:POOL_PAD, :] = zeros
    pos = lax.broadcasted_iota(jnp.int32, (t, gw), 0)
    for gi, win in enumerate(POOL_WINDOWS):
        half = win // 2
        u = p_ref[0, :, gi * gw:(gi + 1) * gw].astype(F32)
        ext = t + POOL_PAD
        acc = jnp.concatenate([u, zeros], axis=0)
        span = 1
        while span < win:
            pad_ref[POOL_PAD:POOL_PAD + ext, :] = acc
            acc = acc + pad_ref[POOL_PAD - span:POOL_PAD - span + ext, :]
            span *= 2
        pad_ref[POOL_PAD:POOL_PAD + ext, :] = acc
        wsum = pad_ref[POOL_PAD + half - 1:POOL_PAD + half - 1 + t, :]
        cnt = (jnp.minimum(pos + half, t) - jnp.maximum(pos - half, 0)).astype(F32)
        y = (wsum / cnt - u).astype(BF16)
        z = jnp.dot(y, w_ref[gi], preferred_element_type=F32)
        o_ref[0, :, gi * gw:(gi + 1) * gw] = (z * s_ref[:, gi * gw:(gi + 1) * gw]).astype(o_ref.dtype)


def _pool(p, pool_w, pool_scale, n, t, first_block, name):
    w = p.shape[-1]
    gw = pool_w.shape[-1]
    return pl.pallas_call(
        _pool_kernel,
        grid=(n,),
        in_specs=[
            pl.BlockSpec((1, t, w), lambda b: (0, first_block + b, 0)),
            _resident(pool_w.shape), _resident(pool_scale.shape),
        ],
        out_specs=pl.BlockSpec((1, t, w), lambda b: (0, b, 0)),
        out_shape=jax.ShapeDtypeStruct((1, n * t, w), BF16),
        scratch_shapes=[pltpu.VMEM((t + 2 * POOL_PAD, gw), F32)],
        compiler_params=_cparams(1, 48),
        name=name,
    )(p, pool_w, pool_scale)


def _head_block_mask(shape):
    r = lax.broadcasted_iota(jnp.int32, shape, 0) // NA_HEAD_DIM
    c = lax.broadcasted_iota(jnp.int32, shape, 1) // NA_HEAD_DIM
    return r == c


def _attend(q_blk, key_parts, val_parts, bias_parts):
    nq, g = q_blk.shape
    heads = g // NA_HEAD_DIM
    qbd = jnp.where(_head_block_mask((heads * nq, g)),
                    jnp.concatenate([q_blk] * heads, axis=0), jnp.zeros((), BF16))
    nt = (((1,), (1,)), ((), ()))
    scores = []
    for keys, bias in zip(key_parts, bias_parts):
        s = lax.dot_general(keys, qbd, nt, preferred_element_type=F32)
        scores.append(s if bias is None else s + bias)
    m = functools.reduce(jnp.maximum, [jnp.max(s, axis=0, keepdims=True) for s in scores])
    ps = [jnp.exp2(s - m) for s in scores]
    l = functools.reduce(jnp.add, [jnp.sum(p, axis=0, keepdims=True) for p in ps])
    tn = (((0,), (0,)), ((), ()))
    r = None
    for p, vals in zip(ps, val_parts):
        part = lax.dot_general(p.astype(BF16), vals, tn, preferred_element_type=F32)
        r = part if r is None else r + part
    inv_t = jnp.transpose(jnp.broadcast_to(1.0 / l, (LANES, heads * nq)))
    inv_rows = jnp.concatenate([inv_t] * (g // LANES), axis=1)
    lane_head = lax.broadcasted_iota(jnp.int32, (nq, g), 1) // NA_HEAD_DIM
    out = jnp.zeros((nq, g), F32)
    for h in range(heads):
        rows = slice(h * nq, (h + 1) * nq)
        out = jnp.where(lane_head == h, r[rows, :] * inv_rows[rows, :], out)
    return out


NATTN_UNROLL = 32


def _nattn_kernel(q_ref, k_ref, v_ref, kc_ref, vc_ref, bias_ref, o_ref):
    s = q_ref.shape[1]
    rows = s // GRID_W
    kh = min(NB_ROWS, rows)
    kc = kc_ref[0]
    vc = vc_ref[0]

    def body(i, carry):
        r0 = jnp.clip(i - kh // 2, 0, rows - kh)
        q0 = pl.multiple_of(i * GRID_W, GRID_W)
        k0 = pl.multiple_of(r0 * GRID_W, GRID_W)
        q_blk = q_ref[0, pl.ds(q0, GRID_W), :]
        keys = k_ref[0, pl.ds(k0, kh * GRID_W), :]
        vals = v_ref[0, pl.ds(k0, kh * GRID_W), :]
        bias = bias_ref[0, pl.ds(r0 - i + (NB_ROWS - 1), kh), :, :]
        bias = bias.reshape(kh * GRID_W, bias.shape[-1])
        out = _attend(q_blk, [keys, kc], [vals, vc], [bias, None])
        o_ref[0, pl.ds(q0, GRID_W), :] = out.astype(o_ref.dtype)
        return carry

    lax.fori_loop(0, rows, body, 0, unroll=NATTN_UNROLL)


def _ctx_attn_kernel(q_ref, kc_ref, vc_ref, o_ref):
    t = q_ref.shape[1]
    kc = kc_ref[0]
    vc = vc_ref[0]
    for i in range(t // GRID_W):
        q_blk = q_ref[0, i * GRID_W:(i + 1) * GRID_W, :]
        out = _attend(q_blk, [kc], [vc], [None])
        o_ref[0, i * GRID_W:(i + 1) * GRID_W, :] = out.astype(o_ref.dtype)


def _bias_table(rpb):
    h = rpb.shape[0]
    kcol = np.arange(GRID_W)[:, None]
    qcol = np.arange(GRID_W)[None, :]
    qstart = np.clip(qcol - NB_COLS // 2, 0, GRID_W - NB_COLS)
    mask = (kcol >= qstart) & (kcol < qstart + NB_COLS)
    idx = np.clip(kcol - qcol + NB_COLS - 1, 0, 2 * NB_COLS - 2)
    onehot = (idx[None] == np.arange(rpb.shape[-1])[:, None, None]).astype(np.float32)
    t = jnp.einsum("hrj,jkq->hrkq", rpb.astype(F32), onehot, precision=lax.Precision.HIGHEST)
    t = jnp.where(mask[None, None], t * LOG2_E, -jnp.inf)
    t = t.reshape(h // HEADS_PER_GROUP, HEADS_PER_GROUP, t.shape[1], GRID_W, GRID_W)
    t = jnp.transpose(t, (0, 2, 3, 1, 4))
    return t.reshape(h // HEADS_PER_GROUP, t.shape[1], GRID_W, HEADS_PER_GROUP * GRID_W)


def _nattn(q, k, v, rpb, b, s, n_ctx):
    w = q.shape[-1]
    g = HEADS_PER_GROUP * NA_HEAD_DIM
    bias = _bias_table(rpb)
    ctx0 = (b * s) // n_ctx
    lat = pl.BlockSpec((1, s, g), lambda i, j: (0, i, j))
    ctx = pl.BlockSpec((1, n_ctx, g), lambda i, j: (0, ctx0 + i, j))
    return pl.pallas_call(
        _nattn_kernel,
        grid=(b, w // g),
        in_specs=[lat, lat, lat, ctx, ctx,
                  pl.BlockSpec((1,) + bias.shape[1:], lambda i, j: (j, 0, 0, 0))],
        out_specs=lat,
        out_shape=jax.ShapeDtypeStruct((1, b * s, w), BF16),
        compiler_params=_cparams(2, 48),
        name="nattn",
    )(q, k, v, k, v, bias)


def _ctx_attn(q, k, v, b, s, n_ctx):
    w = q.shape[-1]
    g = HEADS_PER_GROUP * NA_HEAD_DIM
    ctx0 = (b * s) // n_ctx
    ctx = pl.BlockSpec((1, n_ctx, g), lambda i, j: (0, ctx0 + i, j))
    return pl.pallas_call(
        _ctx_attn_kernel,
        grid=(b, w // g),
        in_specs=[ctx, ctx, ctx],
        out_specs=pl.BlockSpec((1, n_ctx, g), lambda i, j: (0, i, j)),
        out_shape=jax.ShapeDtypeStruct((1, b * n_ctx, w), BF16),
        compiler_params=_cparams(2, 32),
        name="ctx_attn",
    )(q, k, v)


RG_CHUNK = 128
HALO = SUBLANES
SCAN_UNROLL = 16
COEF_UNROLL = 2


def _rglru_kernel(rxf_ref, rxf_prev_ref, rxf_next_ref, rxb_ref, rxb_prev_ref, rxb_next_ref, rc_ref,
                  cw_ref, cb_ref, gw_ref, lam_ref, hf_ref, hb_ref,
                  win_ref, af_ref, bf_ref, ab_ref, bb_ref, sf_ref, sb_ref, state_ref, *, n_ctx):
    nbatch, nblk, tc, _ = rxf_ref.shape
    ngroups = nblk // COEF_UNROLL
    ctx_chunks = n_ctx // tc
    x_chunks = pl.num_programs(0) - ctx_chunks
    c = pl.program_id(0)
    in_ctx = c < ctx_chunks

    @pl.when(c == 0)
    def _():
        state_ref[...] = jnp.zeros(state_ref.shape, F32)

    lane = lax.broadcasted_iota(jnp.int32, (nbatch * tc, RG_BW), 1)
    bias_lanes = (lane < 2).astype(F32).astype(BF16)

    def fill_from_ctx(win, n, j):
        for b in range(nbatch):
            base = b * n_ctx + j * tc
            prev0 = pl.multiple_of(jnp.maximum(base - HALO, b * n_ctx), HALO)
            next0 = pl.multiple_of(jnp.minimum(base + tc, (b + 1) * n_ctx - HALO), HALO)
            win[b, 0:HALO, :] = jnp.where(j > 0, rc_ref[n, pl.ds(prev0, HALO), :], 0.0)
            win[b, HALO:HALO + tc, :] = rc_ref[n, pl.ds(pl.multiple_of(base, tc), tc), :]
            win[b, HALO + tc:2 * HALO + tc, :] = jnp.where(
                j < ctx_chunks - 1, rc_ref[n, pl.ds(next0, HALO), :], 0.0)

    def fill_from_x(win, n, j, cur_ref, prev_ref, next_ref):
        for b in range(nbatch):
            win[b, 0:HALO, :] = jnp.where(j > 0, prev_ref[b, n], 0.0)
            win[b, HALO:HALO + tc, :] = cur_ref[b, n]
            win[b, HALO + tc:2 * HALO + tc, :] = jnp.where(j < x_chunks - 1, next_ref[b, n], 0.0)

    def coefficients(d, j_ctx, j_x, x_refs, a_ref, b_ref):
        cols = slice(d * 2 * RG_BW, (d + 1) * 2 * RG_BW)

        def fill(n, win):
            pl.when(in_ctx)(lambda: fill_from_ctx(win, n, j_ctx))
            pl.when(jnp.logical_not(in_ctx))(lambda: fill_from_x(win, n, j_x, *x_refs))

        def body(g, p, win):
            n = g * COEF_UNROLL + p
            cw = cw_ref[n]
            hus = []
            for b in range(nbatch):
                y = cb_ref[n]
                for k in range(CONV_W):
                    lo = HALO + k - CONV_W // 2
                    y = y + win[b, lo:lo + tc, :] * cw[k:k + 1, :]
                hus.append(y)
            hu = jnp.concatenate(hus, axis=0)
            lhs = jnp.concatenate([hu.astype(BF16), bias_lanes], axis=1)
            z = jnp.dot(lhs, gw_ref[n, :, cols], preferred_element_type=F32)
            t_r = jnp.tanh(z[:, :RG_BW])
            t_i = jnp.tanh(z[:, RG_BW:])
            lam = -lam_ref[n, d:d + 1, :]
            softplus = jnp.maximum(lam, 0.0) + jnp.log1p(jnp.exp(-jnp.abs(lam)))
            log_a = (t_r + 1.0) * ((-0.5 * RG_C) * softplus)
            a = jnp.exp(log_a)
            th = jnp.tanh(log_a)
            num = -2.0 * th
            root = jnp.where(num > 0.0, num * lax.rsqrt(num * (1.0 - th)), 0.0)
            coef = root * ((t_i + 1.0) * hu)
            for b in range(nbatch):
                rows = pl.ds(p * nbatch + b, tc, stride=SUBLANES)
                a_ref[g, rows, :] = a[b * tc:(b + 1) * tc]
                b_ref[g, rows, :] = coef[b * tc:(b + 1) * tc]

        return fill, body

    xf = c - ctx_chunks
    stages = [coefficients(0, c, xf, (rxf_ref, rxf_prev_ref, rxf_next_ref), af_ref, bf_ref),
              coefficients(1, ctx_chunks - 1 - c, x_chunks - 1 - xf, (rxb_ref, rxb_prev_ref, rxb_next_ref),
                           ab_ref, bb_ref)]

    def group(g, carry):
        for d, (fill, _) in enumerate(stages):
            for p in range(COEF_UNROLL):
                fill(g * COEF_UNROLL + p, win_ref.at[d * COEF_UNROLL + p])
        for d, (_, body) in enumerate(stages):
            for p in range(COEF_UNROLL):
                body(g, p, win_ref.at[d * COEF_UNROLL + p])
        return carry

    lax.fori_loop(0, ngroups, group, 0)

    def scan_step(t, hs):
        out_f, out_b = [], []
        rows_f = pl.ds(pl.multiple_of(t * SUBLANES, SUBLANES), SUBLANES)
        rows_b = pl.ds(pl.multiple_of((tc - 1 - t) * SUBLANES, SUBLANES), SUBLANES)
        for g in range(ngroups):
            h = af_ref[g, rows_f, :] * hs[g] + bf_ref[g, rows_f, :]
            sf_ref[g, rows_f, :] = h
            out_f.append(h)
            h = ab_ref[g, rows_b, :] * hs[ngroups + g] + bb_ref[g, rows_b, :]
            sb_ref[g, rows_b, :] = h
            out_b.append(h)
        return tuple(out_f + out_b)

    hs = tuple(state_ref[d, g] for d in range(2) for g in range(ngroups))
    hs = lax.fori_loop(0, tc, scan_step, hs, unroll=SCAN_UNROLL)
    for d in range(2):
        for g in range(ngroups):
            state_ref[d, g] = hs[d * ngroups + g]

    @pl.when(jnp.logical_not(in_ctx))
    def _():
        def emit(g, carry):
            for p in range(COEF_UNROLL):
                for b in range(nbatch):
                    rows = pl.ds(p * nbatch + b, tc, stride=SUBLANES)
                    hf_ref[b, g * COEF_UNROLL + p] = sf_ref[g, rows, :].astype(hf_ref.dtype)
                    hb_ref[b, g * COEF_UNROLL + p] = sb_ref[g, rows, :].astype(hb_ref.dtype)
            return carry
        lax.fori_loop(0, ngroups, emit, 0)


def _rglru(rx, rc, n_ctx, conv_w, conv_b, gate_w, gate_b, lam):
    b, nb, s, _ = rx.shape
    tc = RG_CHUNK
    ctx_chunks = n_ctx // tc
    x_chunks = s // tc
    per_halo = tc // HALO
    gw = jnp.transpose(gate_w, (2, 3, 0, 1, 4)).reshape(nb, RG_BW, 4 * RG_BW).astype(BF16)
    gb = 0.5 * jnp.transpose(gate_b.reshape(2, 2, nb, RG_BW), (2, 0, 1, 3)).reshape(nb, 1, 4 * RG_BW)
    gb_hi = gb.astype(BF16)
    gb_lo = (gb - gb_hi.astype(F32)).astype(BF16)
    pad = jnp.zeros((nb, MXU_DIM - RG_BW - 2, 4 * RG_BW), BF16)
    gw = jnp.concatenate([gw, gb_hi, gb_lo, pad], axis=1)
    cw = 0.5 * jnp.transpose(conv_w.reshape(CONV_W, nb, RG_BW), (1, 0, 2))
    cb = 0.5 * conv_b.reshape(nb, 1, RG_BW)
    lam_s = jnp.transpose(lam.reshape(2, nb, RG_BW), (1, 0, 2))

    chunk_f = lambda c: jnp.maximum(c - ctx_chunks, 0)
    chunk_b = lambda c: jnp.minimum(x_chunks - 1 + ctx_chunks - c, x_chunks - 1)
    cur = lambda f: pl.BlockSpec((b, nb, tc, LANES), lambda c: (0, 0, f(c), 0))
    prev = lambda f: pl.BlockSpec((b, nb, HALO, LANES),
                                  lambda c: (0, 0, jnp.maximum(f(c) * per_halo - 1, 0), 0))
    nxt = lambda f: pl.BlockSpec((b, nb, HALO, LANES),
                                 lambda c: (0, 0, jnp.minimum((f(c) + 1) * per_halo, s // HALO - 1), 0))
    assert COEF_UNROLL * b == SUBLANES and nb % COEF_UNROLL == 0, (b, nb)
    ngroups = nb // COEF_UNROLL
    slab = pltpu.VMEM((ngroups, tc * SUBLANES, LANES), F32)
    window = pltpu.VMEM((2 * COEF_UNROLL, b, tc + 2 * HALO, LANES), F32)
    state_shape = jax.ShapeDtypeStruct((b, nb, s, LANES), BF16)
    return pl.pallas_call(
        functools.partial(_rglru_kernel, n_ctx=n_ctx),
        grid=(ctx_chunks + x_chunks,),
        in_specs=[cur(chunk_f), prev(chunk_f), nxt(chunk_f), cur(chunk_b), prev(chunk_b), nxt(chunk_b),
                  _resident(rc.shape), _resident(cw.shape), _resident(cb.shape), _resident(gw.shape),
                  _resident(lam_s.shape)],
        out_specs=[cur(chunk_f), cur(chunk_b)],
        out_shape=[state_shape, state_shape],
        scratch_shapes=[window, slab, slab, slab, slab, slab, slab,
                        pltpu.VMEM((2, ngroups, SUBLANES, LANES), F32)],
        compiler_params=_cparams(1, 52),
        name="rglru",
    )(rx, rx, rx, rx, rx, rx, rc, cw, cb, gw, lam_s)


def kernel(x, c, ctx, c_ctx, ada_w, ada_b, norm_g, ffn_w_gu, ffn_w_down, ab_w_in, ab_w_out, pool_w,
           pool_scale, na_rpb, rg_w_in, rg_conv_w, rg_conv_b, rg_gate_w, rg_gate_b, rg_lambda,
           rg_w_out, final_g):
    b, s, d = x.shape
    n_ctx = ctx.shape[1]
    depth = ada_w.shape[0]
    pool_width = pool_w.shape[1] * pool_w.shape[2]
    na_width = (ab_w_in.shape[-1] - pool_width) // 3
    d_rnn = rg_w_out.shape[1]
    na_scale = NA_HEAD_DIM ** -0.5 * LOG2_E

    mods = _mods(c, c_ctx, ada_w, ada_b)
    tg = _TokenGrid(b, s, n_ctx)
    tg_x = _TokenGrid(b, s, n_ctx, with_ctx=False)
    xs = [x.reshape(1, b * s, d), ctx.reshape(1, b * n_ctx, d)]

    for i in range(depth):
        last = i == depth - 1
        mod = mods[i, :b + 1]
        g = norm_g[i]
        j = i // 2
        fg = final_g.reshape(1, d) if last else None
        if i % 2 == 0:
            pw = pool_w[j].astype(BF16)
            ps = pool_scale[j].reshape(1, pool_width)
            splits = ((pool_width, 1.0, False), (na_width, na_scale, False), (na_width, 1.0, False),
                      (na_width, 1.0, False))
            dts = (BF16, BF16, BF16, BF16)
            xm, p, q, k, v = _pre(tg, xs, mod, g, ffn_w_gu, ffn_w_down, ab_w_in, (i, j), splits, dts,
                                  f"pre{i}")
            a_x = _pool(p, pw, ps, b, s, 0, f"pool{i}_x")
            b_x = _nattn(q, k, v, na_rpb[j], b, s, n_ctx)
            if last:
                raise NotImplementedError("a final pooling/attention layer is not needed at this depth")
            a_c = _pool(p, pw, ps, b, n_ctx, (b * s) // n_ctx, f"pool{i}_c")
            b_c = _ctx_attn(q, k, v, b, s, n_ctx)
            xm = _post(tg, xm, mod, g, [(a_x, a_c), (b_x, b_c)], ab_w_out, ffn_w_gu, ffn_w_down, (i, j),
                       fg, f"post{i}")
        else:
            splits = ((d_rnn, 1.0, False), (d_rnn, 1.0, True))
            dts = (BF16, F32)
            xm, gate, (rx, rc) = _pre(tg, xs, mod, g, ffn_w_gu, ffn_w_down, rg_w_in, (i, j), splits, dts,
                                      f"pre{i}")
            hf, hb = _rglru(rx, rc[0], n_ctx, rg_conv_w[j], rg_conv_b[j], rg_gate_w[j], rg_gate_b[j],
                            rg_lambda[j])
            if not last:
                raise NotImplementedError("context output of an RG-LRU layer is not needed at this depth")
            xm = _post(tg_x, xm, mod, g, (gate, hf, hb), rg_w_out, ffn_w_gu, ffn_w_down, (i, j), fg,
                       f"post{i}", gated_scan=True)
        xs = [xm]
    return xm.reshape(b, s, d)
```

```python
import functools

import numpy as np
import jax
import jax.numpy as jnp
from jax import lax
from jax.experimental import pallas as pl
from jax.experimental.pallas import tpu as pltpu

F32 = jnp.float32
BF16 = jnp.bfloat16

N_MOD = 9
RMS_EPS = 1e-6
GRID_W = 64
POOL_WINDOWS = (2, 4, 8, 16)
NA_HEAD_DIM = 64
NB_ROWS = 8
NB_COLS = 16
RG_BW = 128
CONV_W = 4
RG_C = 8.0
LOG2_E = float(np.log2(np.e))

LANES = 128
SUBLANES = 8
MXU_DIM = 256
VMEM_BYTES = 64 * 1024 * 1024

HEADS_PER_GROUP = MXU_DIM // NA_HEAD_DIM
TOKEN_TILE = 512
FF_CHUNK = MXU_DIM


def _cparams(n_axes, vmem_mb):
    return pltpu.CompilerParams(
        dimension_semantics=("arbitrary",) * n_axes,
        vmem_limit_bytes=min(vmem_mb * 1024 * 1024, VMEM_BYTES - 8 * 1024 * 1024),
    )


def _resident(shape):
    nd = len(shape)
    return pl.BlockSpec(shape, lambda *_: (0,) * nd, pipeline_mode=pl.Buffered(1))


def _silu(x):
    return x * jax.nn.sigmoid(x)


def _gelu_tanh(x):
    c = np.sqrt(2.0 / np.pi).astype(np.float32)
    return 0.5 * x * (1.0 + jnp.tanh(c * (x + 0.044715 * (x * x * x))))


def _rms_mod(x, g, shift, scale):
    ms = jnp.mean(x * x, axis=-1, keepdims=True)
    y = (x * lax.rsqrt(ms + RMS_EPS)) * g
    return y * (1.0 + scale) + shift


def _swiglu(h, wgu_ref, wd_ref, act_ref):
    d_ff = wd_ref.shape[0]
    for lo in range(0, d_ff, FF_CHUNK):
        hi = min(lo + FF_CHUNK, d_ff)
        ug = jnp.dot(h, wgu_ref[:, lo:hi], preferred_element_type=F32)
        uu = jnp.dot(h, wgu_ref[:, d_ff + lo:d_ff + hi], preferred_element_type=F32)
        act_ref[:, lo:hi] = (_silu(ug) * uu).astype(BF16)
    return jnp.dot(act_ref[...], wd_ref[...], preferred_element_type=F32)


WEIGHT_CHUNK_ROWS = 128


def _hbm():
    return pl.BlockSpec(memory_space=pl.ANY)


def _load_weights_bf16(jobs, stage_ref, sem):
    rows = WEIGHT_CHUNK_ROWS
    width = stage_ref.shape[-1]
    groups = []
    for src, dst in jobs:
        cols = dst.shape[1]
        chunks = list(range(dst.shape[0] // rows))
        per_slot = width // cols
        groups += [(src, dst, cols, chunks[k:k + per_slot]) for k in range(0, len(chunks), per_slot)]

    def copies(group, slot):
        src, _, cols, chunks = group
        return [pltpu.make_async_copy(src.at[pl.ds(i * rows, rows), :],
                                      stage_ref.at[slot, :, pl.ds(p * cols, cols)], sem.at[slot])
                for p, i in enumerate(chunks)]

    for cp in copies(groups[0], 0):
        cp.start()
    for k, group in enumerate(groups):
        slot = k % 2
        if k + 1 < len(groups):
            for cp in copies(groups[k + 1], 1 - slot):
                cp.start()
        for cp in copies(group, slot):
            cp.wait()
        _, dst, cols, chunks = group
        for p, i in enumerate(chunks):
            dst[i * rows:(i + 1) * rows, :] = stage_ref[slot, :, p * cols:(p + 1) * cols].astype(BF16)


def _weight_scratch(shapes):
    width = max(s[1] for s in shapes)
    return ([pltpu.VMEM(s, BF16) for s in shapes]
            + [pltpu.VMEM((2, WEIGHT_CHUNK_ROWS, width), F32), pltpu.SemaphoreType.DMA((2,))])


def _mods_kernel(c_ref, w_ref, b_ref, o_ref):
    h = _silu(c_ref[...]).astype(BF16)
    w = w_ref[0].astype(BF16)
    o_ref[0] = jnp.dot(h, w, preferred_element_type=F32) + b_ref[0]


def _mods(c, c_ctx, ada_w, ada_b):
    depth, d, n = ada_w.shape
    b = c.shape[0]
    rows = jnp.zeros((SUBLANES, d), F32).at[:b].set(c).at[b].set(c_ctx)
    tn = n // 4
    out = pl.pallas_call(
        _mods_kernel,
        grid=(depth, n // tn),
        in_specs=[
            pl.BlockSpec((SUBLANES, d), lambda i, j: (0, 0)),
            pl.BlockSpec((1, d, tn), lambda i, j: (i, 0, j)),
            pl.BlockSpec((1, 1, tn), lambda i, j: (i, 0, j)),
        ],
        out_specs=pl.BlockSpec((1, SUBLANES, tn), lambda i, j: (i, 0, j)),
        out_shape=jax.ShapeDtypeStruct((depth, SUBLANES, n), F32),
        compiler_params=_cparams(2, 40),
        name="mods",
    )(rows, ada_w, ada_b.reshape(depth, 1, n))
    return out.reshape(depth, SUBLANES, N_MOD, d)


class _TokenGrid:
    def __init__(self, n_batch, seq, n_ctx, with_ctx=True):
        self.tm = TOKEN_TILE
        self.n_batch = n_batch
        self.per_batch = seq // self.tm
        self.x_tiles = n_batch * self.per_batch
        self.c_tiles = (n_batch * n_ctx) // self.tm if with_ctx else 0
        self.n_x = n_batch * seq
        self.n_tok = self.n_x + (n_batch * n_ctx if with_ctx else 0)

    @property
    def grid(self):
        return (self.x_tiles + self.c_tiles,)

    def _x_tile(self, i):
        return jnp.minimum(i, self.x_tiles - 1)

    def _c_tile(self, i):
        return jnp.maximum(i - self.x_tiles, 0)

    def merged(self, width):
        return pl.BlockSpec((1, self.tm, width), lambda i: (0, i, 0))

    def x_only(self, width):
        return pl.BlockSpec((1, self.tm, width), lambda i: (0, self._x_tile(i), 0))

    def c_only(self, width):
        return pl.BlockSpec((1, self.tm, width), lambda i: (0, self._c_tile(i), 0))

    def x_slab(self, n_slabs):
        def index(i):
            t = self._x_tile(i)
            return (t // self.per_batch, 0, t % self.per_batch, 0)
        return pl.BlockSpec((1, n_slabs, self.tm, LANES), index)

    def c_slab(self, n_slabs):
        return pl.BlockSpec((1, n_slabs, self.tm, LANES), lambda i: (0, 0, self._c_tile(i), 0))

    def mod(self, d):
        return pl.BlockSpec((1, N_MOD, d), lambda i: (jnp.minimum(i // self.per_batch, self.n_batch), 0, 0))

    def is_x(self, step):
        return step < self.x_tiles


def _pre_kernel(*refs, splits, n_src, x_tiles, w_index):
    x_refs, mod_ref = refs[:n_src], refs[n_src]
    g_ref, wgu_hbm, wd_hbm, win_hbm = refs[n_src + 1:n_src + 5]
    pos = n_src + 5
    x_out = refs[pos]
    pos += 1
    out_refs = []
    for _, _, slab in splits:
        out_refs.append(refs[pos:pos + (2 if slab else 1)])
        pos += 2 if slab else 1
    act_ref, wgu_ref, wd_ref, win_ref, stage_ref, sem = refs[pos:]
    step = pl.program_id(0)
    layer, mixer = w_index
    is_x = step < x_tiles

    @pl.when(step == 0)
    def _():
        _load_weights_bf16([(wgu_hbm.at[layer, 0], wgu_ref), (wd_hbm.at[layer, 0], wd_ref),
                            (win_hbm.at[mixer], win_ref)], stage_ref, sem)

    x = x_refs[0][0] if n_src == 1 else jnp.where(is_x, x_refs[0][0], x_refs[1][0])
    mod = mod_ref[0]
    h = _rms_mod(x, g_ref[0:1, :], mod[0:1, :], mod[1:2, :]).astype(BF16)
    y = _swiglu(h, wgu_ref, wd_ref, act_ref)
    x = x + (0.5 * mod[2:3, :]) * y
    x_out[0] = x
    h = _rms_mod(x, g_ref[1:2, :], mod[3:4, :], mod[4:5, :]).astype(BF16)
    off = 0
    for o_refs, (width, scale, slab) in zip(out_refs, splits):
        u = jnp.dot(h, win_ref[:, off:off + width], preferred_element_type=F32)
        if scale != 1.0:
            u = u * scale
        off += width
        if not slab:
            o_refs[0][0] = u.astype(o_refs[0].dtype)
            continue

        def write(o_ref, u=u, width=width):
            for j in range(width // LANES):
                o_ref[0, j] = u[:, j * LANES:(j + 1) * LANES].astype(o_ref.dtype)

        pl.when(is_x)(functools.partial(write, o_refs[0]))
        pl.when(jnp.logical_not(is_x))(functools.partial(write, o_refs[1]))


def _pre(tg, xs, mod, g, wgu, wd, win, w_index, splits, dtypes, name):
    d = xs[0].shape[-1]
    n_src = len(xs)
    in_specs = [tg.merged(d)] if n_src == 1 else [tg.x_only(d), tg.c_only(d)]
    in_specs += [tg.mod(d), _resident(g.shape), _hbm(), _hbm(), _hbm()]
    out_shape = [jax.ShapeDtypeStruct((1, tg.n_tok, d), F32)]
    out_specs = [tg.merged(d)]
    for (w, _, is_slab), dt in zip(splits, dtypes):
        if is_slab:
            k = w // LANES
            out_shape += [jax.ShapeDtypeStruct((tg.n_batch, k, tg.n_x // tg.n_batch, LANES), dt),
                          jax.ShapeDtypeStruct((1, k, tg.n_tok - tg.n_x, LANES), dt)]
            out_specs += [tg.x_slab(k), tg.c_slab(k)]
        else:
            out_shape.append(jax.ShapeDtypeStruct((1, tg.n_tok, w), dt))
            out_specs.append(tg.merged(w))
    w_shapes = [wgu.shape[-2:], wd.shape[-2:], win.shape[-2:]]
    outs = pl.pallas_call(
        functools.partial(_pre_kernel, splits=splits, n_src=n_src, x_tiles=tg.x_tiles, w_index=w_index),
        grid=tg.grid,
        in_specs=in_specs,
        out_specs=out_specs,
        out_shape=out_shape,
        scratch_shapes=[pltpu.VMEM((tg.tm, wd.shape[-2]), BF16)] + _weight_scratch(w_shapes),
        compiler_params=_cparams(1, 56),
        name=name,
    )(*xs, mod, g, wgu, wd, win)
    result, pos = [outs[0]], 1
    for _, _, is_slab in splits:
        result.append(tuple(outs[pos:pos + 2]) if is_slab else outs[pos])
        pos += 2 if is_slab else 1
    return result


def _post_kernel(*refs, n_parts, final, gated_scan, x_tiles, w_index):
    x_ref, mod_ref, g_ref = refs[:3]
    pos = 3
    if gated_scan:
        part_refs = refs[pos:pos + 3]
        pos += 3
    else:
        part_refs = [refs[pos + 2 * j:pos + 2 * j + 2] for j in range(n_parts)]
        pos += 2 * n_parts
    wout_hbm, wgu_hbm, wd_hbm = refs[pos:pos + 3]
    pos += 3
    if final:
        gf_ref = refs[pos]
        pos += 1
    o_ref, act_ref = refs[pos:pos + 2]
    pos += 2
    if gated_scan:
        mix_ref = refs[pos]
        pos += 1
    wout_ref, wgu_ref, wd_ref, stage_ref, sem = refs[pos:]
    step = pl.program_id(0)
    layer, mixer = w_index

    @pl.when(step == 0)
    def _():
        _load_weights_bf16([(wout_hbm.at[mixer], wout_ref), (wgu_hbm.at[layer, 1], wgu_ref),
                            (wd_hbm.at[layer, 1], wd_ref)], stage_ref, sem)

    if gated_scan:
        gate_ref, hf_ref, hb_ref = part_refs
        per_dot = MXU_DIM // LANES
        acc = None
        for j0 in range(0, hf_ref.shape[1], per_dot):
            for j in range(j0, j0 + per_dot):
                gate = gate_ref[0, :, j * LANES:(j + 1) * LANES].astype(F32)
                hsum = hf_ref[0, j].astype(F32) + hb_ref[0, j].astype(F32)
                mix_ref[:, j * LANES:(j + 1) * LANES] = (_gelu_tanh(gate) * hsum).astype(BF16)
            k = slice(j0 * LANES, (j0 + per_dot) * LANES)
            part = jnp.dot(mix_ref[:, k], wout_ref[k, :], preferred_element_type=F32)
            acc = part if acc is None else acc + part
    else:
        is_x = step < x_tiles
        acc = None
        off = 0
        for px_ref, pc_ref in part_refs:
            y = jnp.where(is_x, px_ref[0], pc_ref[0])
            w = y.shape[-1]
            part = jnp.dot(y, wout_ref[off:off + w, :], preferred_element_type=F32)
            acc = part if acc is None else acc + part
            off += w
    mod = mod_ref[0]
    x = x_ref[0] + mod[5:6, :] * acc
    h = _rms_mod(x, g_ref[2:3, :], mod[6:7, :], mod[7:8, :]).astype(BF16)
    y = _swiglu(h, wgu_ref, wd_ref, act_ref)
    x = x + (0.5 * mod[8:9, :]) * y
    if final:
        ms = jnp.mean(x * x, axis=-1, keepdims=True)
        x = (x * lax.rsqrt(ms + RMS_EPS)) * gf_ref[...]
    o_ref[0] = x


def _post(tg, x, mod, g, parts, wout, wgu, wd, w_index, final_g, name, gated_scan=False):
    d = x.shape[-1]
    args = [x, mod, g]
    in_specs = [tg.merged(d), tg.mod(d), _resident(g.shape)]
    if gated_scan:
        gate, hf, hb = parts
        args += [gate, hf, hb]
        in_specs += [tg.merged(gate.shape[-1]), tg.x_slab(hf.shape[1]), tg.x_slab(hb.shape[1])]
    else:
        for px, pc in parts:
            args += [px, pc]
            in_specs += [tg.x_only(px.shape[-1]), tg.c_only(pc.shape[-1])]
    args += [wout, wgu, wd]
    in_specs += [_hbm(), _hbm(), _hbm()]
    if final_g is not None:
        args.append(final_g)
        in_specs.append(_resident(final_g.shape))
    scratch = [pltpu.VMEM((tg.tm, wd.shape[-2]), BF16)]
    if gated_scan:
        scratch.append(pltpu.VMEM((tg.tm, wout.shape[-2]), BF16))
    scratch += _weight_scratch([wout.shape[-2:], wgu.shape[-2:], wd.shape[-2:]])
    return pl.pallas_call(
        functools.partial(_post_kernel, n_parts=len(parts), final=final_g is not None,
                          gated_scan=gated_scan, x_tiles=tg.x_tiles, w_index=w_index),
        grid=tg.grid,
        in_specs=in_specs,
        out_specs=tg.merged(d),
        out_shape=jax.ShapeDtypeStruct((1, tg.n_tok, d), F32),
        scratch_shapes=scratch,
        compiler_params=_cparams(1, 56),
        name=name,
    )(*args)


POOL_PAD = SUBLANES * 2


def _pool_kernel(p_ref, w_ref, s_ref, o_ref, pad_ref):
    t = p_ref.shape[1]
    gw = w_ref.shape[-1]
    zeros = jnp.zeros((POOL_PAD, gw), F32)
    pad_ref[0:POOL_PAD, :] = zeros
    pos = lax.broadcasted_iota(jnp.int32, (t, gw), 0)
    for gi, win in enumerate(POOL_WINDOWS):
        half = win // 2
        u = p_ref[0, :, gi * gw:(gi + 1) * gw].astype(F32)
        ext = t + POOL_PAD
        acc = jnp.concatenate([u, zeros], axis=0)
        span = 1
        while span < win:
            pad_ref[POOL_PAD:POOL_PAD + ext, :] = acc
            acc = acc + pad_ref[POOL_PAD - span:POOL_PAD - span + ext, :]
            span *= 2
        pad_ref[POOL_PAD:POOL_PAD + ext, :] = acc
        wsum = pad_ref[POOL_PAD + half - 1:POOL_PAD + half - 1 + t, :]
        cnt = (jnp.minimum(pos + half, t) - jnp.maximum(pos - half, 0)).astype(F32)
        y = (wsum / cnt - u).astype(BF16)
        z = jnp.dot(y, w_ref[gi], preferred_element_type=F32)
        o_ref[0, :, gi * gw:(gi + 1) * gw] = (z * s_ref[:, gi * gw:(gi + 1) * gw]).astype(o_ref.dtype)


def _pool(p, pool_w, pool_scale, n, t, first_block, name):
    w = p.shape[-1]
    gw = pool_w.shape[-1]
    return pl.pallas_call(
        _pool_kernel,
        grid=(n,),
        in_specs=[
            pl.BlockSpec((1, t, w), lambda b: (0, first_block + b, 0)),
            _resident(pool_w.shape), _resident(pool_scale.shape),
        ],
        out_specs=pl.BlockSpec((1, t, w), lambda b: (0, b, 0)),
        out_shape=jax.ShapeDtypeStruct((1, n * t, w), BF16),
        scratch_shapes=[pltpu.VMEM((t + 2 * POOL_PAD, gw), F32)],
        compiler_params=_cparams(1, 48),
        name=name,
    )(p, pool_w, pool_scale)


def _head_block_mask(shape):
    r = lax.broadcasted_iota(jnp.int32, shape, 0) // NA_HEAD_DIM
    c = lax.broadcasted_iota(jnp.int32, shape, 1) // NA_HEAD_DIM
    return r == c


def _attend(q_blk, key_parts, val_parts, bias_parts):
    nq, g = q_blk.shape
    heads = g // NA_HEAD_DIM
    qbd = jnp.where(_head_block_mask((heads * nq, g)),
                    jnp.concatenate([q_blk] * heads, axis=0), jnp.zeros((), BF16))
    nt = (((1,), (1,)), ((), ()))
    scores = []
    for keys, bias in zip(key_parts, bias_parts):
        s = lax.dot_general(keys, qbd, nt, preferred_element_type=F32)
        scores.append(s if bias is None else s + bias)
    m = functools.reduce(jnp.maximum, [jnp.max(s, axis=0, keepdims=True) for s in scores])
    ps = [jnp.exp2(s - m) for s in scores]
    l = functools.reduce(jnp.add, [jnp.sum(p, axis=0, keepdims=True) for p in ps])
    tn = (((0,), (0,)), ((), ()))
    r = None
    for p, vals in zip(ps, val_parts):
        part = lax.dot_general(p.astype(BF16), vals, tn, preferred_element_type=F32)
        r = part if r is None else r + part
    inv_t = jnp.transpose(jnp.broadcast_to(1.0 / l, (LANES, heads * nq)))
    inv_rows = jnp.concatenate([inv_t] * (g // LANES), axis=1)
    lane_head = lax.broadcasted_iota(jnp.int32, (nq, g), 1) // NA_HEAD_DIM
    out = jnp.zeros((nq, g), F32)
    for h in range(heads):
        rows = slice(h * nq, (h + 1) * nq)
        out = jnp.where(lane_head == h, r[rows, :] * inv_rows[rows, :], out)
    return out


NATTN_UNROLL = 32


def _nattn_kernel(q_ref, k_ref, v_ref, kc_ref, vc_ref, bias_ref, o_ref):
    s = q_ref.shape[1]
    rows = s // GRID_W
    kh = min(NB_ROWS, rows)
    kc = kc_ref[0]
    vc = vc_ref[0]

    def body(i, carry):
        r0 = jnp.clip(i - kh // 2, 0, rows - kh)
        q0 = pl.multiple_of(i * GRID_W, GRID_W)
        k0 = pl.multiple_of(r0 * GRID_W, GRID_W)
        q_blk = q_ref[0, pl.ds(q0, GRID_W), :]
        keys = k_ref[0, pl.ds(k0, kh * GRID_W), :]
        vals = v_ref[0, pl.ds(k0, kh * GRID_W), :]
        bias = bias_ref[0, pl.ds(r0 - i + (NB_ROWS - 1), kh), :, :]
        bias = bias.reshape(kh * GRID_W, bias.shape[-1])
        out = _attend(q_blk, [keys, kc], [vals, vc], [bias, None])
        o_ref[0, pl.ds(q0, GRID_W), :] = out.astype(o_ref.dtype)
        return carry

    lax.fori_loop(0, rows, body, 0, unroll=NATTN_UNROLL)


def _ctx_attn_kernel(q_ref, kc_ref, vc_ref, o_ref):
    t = q_ref.shape[1]
    kc = kc_ref[0]
    vc = vc_ref[0]
    for i in range(t // GRID_W):
        q_blk = q_ref[0, i * GRID_W:(i + 1) * GRID_W, :]
        out = _attend(q_blk, [kc], [vc], [None])
        o_ref[0, i * GRID_W:(i + 1) * GRID_W, :] = out.astype(o_ref.dtype)


def _bias_table(rpb):
    h = rpb.shape[0]
    kcol = np.arange(GRID_W)[:, None]
    qcol = np.arange(GRID_W)[None, :]
    qstart = np.clip(qcol - NB_COLS // 2, 0, GRID_W - NB_COLS)
    mask = (kcol >= qstart) & (kcol < qstart + NB_COLS)
    idx = np.clip(kcol - qcol + NB_COLS - 1, 0, 2 * NB_COLS - 2)
    onehot = (idx[None] == np.arange(rpb.shape[-1])[:, None, None]).astype(np.float32)
    t = jnp.einsum("hrj,jkq->hrkq", rpb.astype(F32), onehot, precision=lax.Precision.HIGHEST)
    t = jnp.where(mask[None, None], t * LOG2_E, -jnp.inf)
    t = t.reshape(h // HEADS_PER_GROUP, HEADS_PER_GROUP, t.shape[1], GRID_W, GRID_W)
    t = jnp.transpose(t, (0, 2, 3, 1, 4))
    return t.reshape(h // HEADS_PER_GROUP, t.shape[1], GRID_W, HEADS_PER_GROUP * GRID_W)


def _nattn(q, k, v, rpb, b, s, n_ctx):
    w = q.shape[-1]
    g = HEADS_PER_GROUP * NA_HEAD_DIM
    bias = _bias_table(rpb)
    ctx0 = (b * s) // n_ctx
    lat = pl.BlockSpec((1, s, g), lambda i, j: (0, i, j))
    ctx = pl.BlockSpec((1, n_ctx, g), lambda i, j: (0, ctx0 + i, j))
    return pl.pallas_call(
        _nattn_kernel,
        grid=(b, w // g),
        in_specs=[lat, lat, lat, ctx, ctx,
                  pl.BlockSpec((1,) + bias.shape[1:], lambda i, j: (j, 0, 0, 0))],
        out_specs=lat,
        out_shape=jax.ShapeDtypeStruct((1, b * s, w), BF16),
        compiler_params=_cparams(2, 48),
        name="nattn",
    )(q, k, v, k, v, bias)


def _ctx_attn(q, k, v, b, s, n_ctx):
    w = q.shape[-1]
    g = HEADS_PER_GROUP * NA_HEAD_DIM
    ctx0 = (b * s) // n_ctx
    ctx = pl.BlockSpec((1, n_ctx, g), lambda i, j: (0, ctx0 + i, j))
    return pl.pallas_call(
        _ctx_attn_kernel,
        grid=(b, w // g),
        in_specs=[ctx, ctx, ctx],
        out_specs=pl.BlockSpec((1, n_ctx, g), lambda i, j: (0, i, j)),
        out_shape=jax.ShapeDtypeStruct((1, b * n_ctx, w), BF16),
        compiler_params=_cparams(2, 32),
        name="ctx_attn",
    )(q, k, v)


RG_CHUNK = 128
HALO = SUBLANES
SCAN_UNROLL = 16
COEF_UNROLL = 2


def _rglru_kernel(*refs, n_ctx, latent):
    if latent:
        (rxf_ref, rxf_prev_ref, rxf_next_ref, rxb_ref, rxb_prev_ref, rxb_next_ref, init_ref,
         cw_ref, cb_ref, gw_ref, lam_ref, hf_ref, hb_ref,
         win_ref, af_ref, bf_ref, ab_ref, bb_ref, sf_ref, sb_ref, state_ref) = refs
        nbatch, nblk, tc, _ = rxf_ref.shape
    else:
        (rc_ref, cw_ref, cb_ref, gw_ref, lam_ref, final_ref,
         win_ref, af_ref, bf_ref, ab_ref, bb_ref, state_ref) = refs
        nblk, tc = rc_ref.shape[0], RG_CHUNK
        nbatch = rc_ref.shape[1] // n_ctx
    ngroups = nblk // COEF_UNROLL
    n_chunks = pl.num_programs(0)
    c = pl.program_id(0)

    @pl.when(c == 0)
    def _():
        state_ref[...] = init_ref[...] if latent else jnp.zeros(state_ref.shape, F32)

    lane = lax.broadcasted_iota(jnp.int32, (nbatch * tc, RG_BW), 1)
    bias_lanes = (lane < 2).astype(F32).astype(BF16)

    def fill_from_ctx(win, n, j):
        for b in range(nbatch):
            base = b * n_ctx + j * tc
            prev0 = pl.multiple_of(jnp.maximum(base - HALO, b * n_ctx), HALO)
            next0 = pl.multiple_of(jnp.minimum(base + tc, (b + 1) * n_ctx - HALO), HALO)
            win[b, 0:HALO, :] = jnp.where(j > 0, rc_ref[n, pl.ds(prev0, HALO), :], 0.0)
            win[b, HALO:HALO + tc, :] = rc_ref[n, pl.ds(pl.multiple_of(base, tc), tc), :]
            win[b, HALO + tc:2 * HALO + tc, :] = jnp.where(
                j < n_chunks - 1, rc_ref[n, pl.ds(next0, HALO), :], 0.0)

    def fill_from_x(win, n, j, cur_ref, prev_ref, next_ref):
        for b in range(nbatch):
            win[b, 0:HALO, :] = jnp.where(j > 0, prev_ref[b, n], 0.0)
            win[b, HALO:HALO + tc, :] = cur_ref[b, n]
            win[b, HALO + tc:2 * HALO + tc, :] = jnp.where(j < n_chunks - 1, next_ref[b, n], 0.0)

    def coefficients(d, j, x_refs, a_ref, b_ref):
        cols = slice(d * 2 * RG_BW, (d + 1) * 2 * RG_BW)

        def fill(n, win):
            if latent:
                fill_from_x(win, n, j, *x_refs)
            else:
                fill_from_ctx(win, n, j)

        def body(g, p, win):
            n = g * COEF_UNROLL + p
            cw = cw_ref[n]
            hus = []
            for b in range(nbatch):
                y = cb_ref[n]
                for k in range(CONV_W):
                    lo = HALO + k - CONV_W // 2
                    y = y + win[b, lo:lo + tc, :] * cw[k:k + 1, :]
                hus.append(y)
            hu = jnp.concatenate(hus, axis=0)
            lhs = jnp.concatenate([hu.astype(BF16), bias_lanes], axis=1)
            z = jnp.dot(lhs, gw_ref[n, :, cols], preferred_element_type=F32)
            t_r = jnp.tanh(z[:, :RG_BW])
            t_i = jnp.tanh(z[:, RG_BW:])
            lam = -lam_ref[n, d:d + 1, :]
            softplus = jnp.maximum(lam, 0.0) + jnp.log1p(jnp.exp(-jnp.abs(lam)))
            log_a = (t_r + 1.0) * ((-0.5 * RG_C) * softplus)
            a = jnp.exp(log_a)
            th = jnp.tanh(log_a)
            num = -2.0 * th
            root = jnp.where(num > 0.0, num * lax.rsqrt(num * (1.0 - th)), 0.0)
            coef = root * ((t_i + 1.0) * hu)
            for b in range(nbatch):
                rows = pl.ds(p * nbatch + b, tc, stride=SUBLANES)
                a_ref[g, rows, :] = a[b * tc:(b + 1) * tc]
                b_ref[g, rows, :] = coef[b * tc:(b + 1) * tc]

        return fill, body

    x_refs_f = (rxf_ref, rxf_prev_ref, rxf_next_ref) if latent else None
    x_refs_b = (rxb_ref, rxb_prev_ref, rxb_next_ref) if latent else None
    stages = [coefficients(0, c, x_refs_f, af_ref, bf_ref),
              coefficients(1, n_chunks - 1 - c, x_refs_b, ab_ref, bb_ref)]

    def group(g, carry):
        for d, (fill, _) in enumerate(stages):
            for p in range(COEF_UNROLL):
                fill(g * COEF_UNROLL + p, win_ref.at[d * COEF_UNROLL + p])
        for d, (_, body) in enumerate(stages):
            for p in range(COEF_UNROLL):
                body(g, p, win_ref.at[d * COEF_UNROLL + p])
        return carry

    lax.fori_loop(0, ngroups, group, 0, unroll=True)

    def scan_step(t, hs):
        out_f, out_b = [], []
        rows_f = pl.ds(pl.multiple_of(t * SUBLANES, SUBLANES), SUBLANES)
        rows_b = pl.ds(pl.multiple_of((tc - 1 - t) * SUBLANES, SUBLANES), SUBLANES)
        for g in range(ngroups):
            h = af_ref[g, rows_f, :] * hs[g] + bf_ref[g, rows_f, :]
            if latent:
                sf_ref[g, rows_f, :] = h
            out_f.append(h)
            h = ab_ref[g, rows_b, :] * hs[ngroups + g] + bb_ref[g, rows_b, :]
            if latent:
                sb_ref[g, rows_b, :] = h
            out_b.append(h)
        return tuple(out_f + out_b)

    hs = tuple(state_ref[d, g] for d in range(2) for g in range(ngroups))
    hs = lax.fori_loop(0, tc, scan_step, hs, unroll=SCAN_UNROLL)
    for d in range(2):
        for g in range(ngroups):
            state_ref[d, g] = hs[d * ngroups + g]

    if not latent:
        final_ref[...] = state_ref[...]
        return

    def emit(g, carry):
        for p in range(COEF_UNROLL):
            for b in range(nbatch):
                rows = pl.ds(p * nbatch + b, tc, stride=SUBLANES)
                hf_ref[b, g * COEF_UNROLL + p] = sf_ref[g, rows, :].astype(hf_ref.dtype)
                hb_ref[b, g * COEF_UNROLL + p] = sb_ref[g, rows, :].astype(hb_ref.dtype)
        return carry
    lax.fori_loop(0, ngroups, emit, 0, unroll=True)


def _rglru(rx, rc, n_ctx, conv_w, conv_b, gate_w, gate_b, lam):
    b, nb, s, _ = rx.shape
    tc = RG_CHUNK
    ctx_chunks = n_ctx // tc
    x_chunks = s // tc
    per_halo = tc // HALO
    gw = jnp.transpose(gate_w, (2, 3, 0, 1, 4)).reshape(nb, RG_BW, 4 * RG_BW).astype(BF16)
    gb = 0.5 * jnp.transpose(gate_b.reshape(2, 2, nb, RG_BW), (2, 0, 1, 3)).reshape(nb, 1, 4 * RG_BW)
    gb_hi = gb.astype(BF16)
    gb_lo = (gb - gb_hi.astype(F32)).astype(BF16)
    pad = jnp.zeros((nb, MXU_DIM - RG_BW - 2, 4 * RG_BW), BF16)
    gw = jnp.concatenate([gw, gb_hi, gb_lo, pad], axis=1)
    cw = 0.5 * jnp.transpose(conv_w.reshape(CONV_W, nb, RG_BW), (1, 0, 2))
    cb = 0.5 * conv_b.reshape(nb, 1, RG_BW)
    lam_s = jnp.transpose(lam.reshape(2, nb, RG_BW), (1, 0, 2))

    assert COEF_UNROLL * b == SUBLANES and nb % COEF_UNROLL == 0, (b, nb)
    ngroups = nb // COEF_UNROLL
    slab = pltpu.VMEM((ngroups, tc * SUBLANES, LANES), F32)
    window = pltpu.VMEM((2 * COEF_UNROLL, b, tc + 2 * HALO, LANES), F32)
    carry = jax.ShapeDtypeStruct((2, ngroups, SUBLANES, LANES), F32)
    carry_scratch = pltpu.VMEM(carry.shape, F32)
    params = [cw, cb, gw, lam_s]
    param_specs = [_resident(a.shape) for a in params]

    ctx_state = pl.pallas_call(
        functools.partial(_rglru_kernel, n_ctx=n_ctx, latent=False),
        grid=(ctx_chunks,),
        in_specs=[_resident(rc.shape)] + param_specs,
        out_specs=pl.BlockSpec(carry.shape, lambda c: (0, 0, 0, 0)),
        out_shape=carry,
        scratch_shapes=[window, slab, slab, slab, slab, carry_scratch],
        compiler_params=_cparams(1, 32),
        name="rglru_ctx",
    )(rc, *params)

    chunk_f = lambda c: c
    chunk_b = lambda c: x_chunks - 1 - c
    cur = lambda f: pl.BlockSpec((b, nb, tc, LANES), lambda c: (0, 0, f(c), 0))
    prev = lambda f: pl.BlockSpec((b, nb, HALO, LANES),
                                  lambda c: (0, 0, jnp.maximum(f(c) * per_halo - 1, 0), 0))
    nxt = lambda f: pl.BlockSpec((b, nb, HALO, LANES),
                                 lambda c: (0, 0, jnp.minimum((f(c) + 1) * per_halo, s // HALO - 1), 0))
    state_shape = jax.ShapeDtypeStruct((b, nb, s, LANES), BF16)
    return pl.pallas_call(
        functools.partial(_rglru_kernel, n_ctx=n_ctx, latent=True),
        grid=(x_chunks,),
        in_specs=[cur(chunk_f), prev(chunk_f), nxt(chunk_f), cur(chunk_b), prev(chunk_b), nxt(chunk_b),
                  _resident(carry.shape)] + param_specs,
        out_specs=[cur(chunk_f), cur(chunk_b)],
        out_shape=[state_shape, state_shape],
        scratch_shapes=[window, slab, slab, slab, slab, slab, slab, carry_scratch],
        compiler_params=_cparams(1, 52),
        name="rglru",
    )(rx, rx, rx, rx, rx, rx, ctx_state, *params)


def kernel(x, c, ctx, c_ctx, ada_w, ada_b, norm_g, ffn_w_gu, ffn_w_down, ab_w_in, ab_w_out, pool_w,
           pool_scale, na_rpb, rg_w_in, rg_conv_w, rg_conv_b, rg_gate_w, rg_gate_b, rg_lambda,
           rg_w_out, final_g):
    b, s, d = x.shape
    n_ctx = ctx.shape[1]
    depth = ada_w.shape[0]
    pool_width = pool_w.shape[1] * pool_w.shape[2]
    na_width = (ab_w_in.shape[-1] - pool_width) // 3
    d_rnn = rg_w_out.shape[1]
    na_scale = NA_HEAD_DIM ** -0.5 * LOG2_E

    mods = _mods(c, c_ctx, ada_w, ada_b)
    tg = _TokenGrid(b, s, n_ctx)
    tg_x = _TokenGrid(b, s, n_ctx, with_ctx=False)
    xs = [x.reshape(1, b * s, d), ctx.reshape(1, b * n_ctx, d)]

    for i in range(depth):
        last = i == depth - 1
        mod = mods[i, :b + 1]
        g = norm_g[i]
        j = i // 2
        fg = final_g.reshape(1, d) if last else None
        if i % 2 == 0:
            pw = pool_w[j].astype(BF16)
            ps = pool_scale[j].reshape(1, pool_width)
            splits = ((pool_width, 1.0, False), (na_width, na_scale, False), (na_width, 1.0, False),
                      (na_width, 1.0, False))
            dts = (BF16, BF16, BF16, BF16)
            xm, p, q, k, v = _pre(tg, xs, mod, g, ffn_w_gu, ffn_w_down, ab_w_in, (i, j), splits, dts,
                                  f"pre{i}")
            a_x = _pool(p, pw, ps, b, s, 0, f"pool{i}_x")
            b_x = _nattn(q, k, v, na_rpb[j], b, s, n_ctx)
            if last:
                raise NotImplementedError("a final pooling/attention layer is not needed at this depth")
            a_c = _pool(p, pw, ps, b, n_ctx, (b * s) // n_ctx, f"pool{i}_c")
            b_c = _ctx_attn(q, k, v, b, s, n_ctx)
            xm = _post(tg, xm, mod, g, [(a_x, a_c), (b_x, b_c)], ab_w_out, ffn_w_gu, ffn_w_down, (i, j),
                       fg, f"post{i}")
        else:
            splits = ((d_rnn, 1.0, False), (d_rnn, 1.0, True))
            dts = (BF16, F32)
            xm, gate, (rx, rc) = _pre(tg, xs, mod, g, ffn_w_gu, ffn_w_down, rg_w_in, (i, j), splits, dts,
                                      f"pre{i}")
            hf, hb = _rglru(rx, rc[0], n_ctx, rg_conv_w[j], rg_conv_b[j], rg_gate_w[j], rg_gate_b[j],
                            rg_lambda[j])
            if not last:
                raise NotImplementedError("context output of an RG-LRU layer is not needed at this depth")
            xm = _post(tg_x, xm, mod, g, (gate, hf, hb), rg_w_out, ffn_w_gu, ffn_w_down, (i, j), fg,
                       f"post{i}", gated_scan=True)
        xs = [xm]
    return xm.reshape(b, s, d)
```

```python
import functools

import numpy as np
import jax
import jax.numpy as jnp
from jax import lax
from jax.experimental import pallas as pl
from jax.experimental.pallas import tpu as pltpu

F32 = jnp.float32
BF16 = jnp.bfloat16

N_MOD = 9
RMS_EPS = 1e-6
GRID_W = 64
POOL_WINDOWS = (2, 4, 8, 16)
NA_HEAD_DIM = 64
NB_ROWS = 8
NB_COLS = 16
RG_BW = 128
CONV_W = 4
RG_C = 8.0
LOG2_E = float(np.log2(np.e))

LANES = 128
SUBLANES = 8
MXU_DIM = 256
VMEM_BYTES = 64 * 1024 * 1024

HEADS_PER_GROUP = MXU_DIM // NA_HEAD_DIM
TOKEN_TILE = 512
FF_CHUNK = MXU_DIM


def _cparams(n_axes, vmem_mb):
    return pltpu.CompilerParams(
        dimension_semantics=("arbitrary",) * n_axes,
        vmem_limit_bytes=min(vmem_mb * 1024 * 1024, VMEM_BYTES - 8 * 1024 * 1024),
    )


def _resident(shape):
    nd = len(shape)
    return pl.BlockSpec(shape, lambda *_: (0,) * nd, pipeline_mode=pl.Buffered(1))


def _silu(x):
    return x * jax.nn.sigmoid(x)


def _gelu_tanh(x):
    c = np.sqrt(2.0 / np.pi).astype(np.float32)
    return 0.5 * x * (1.0 + jnp.tanh(c * (x + 0.044715 * (x * x * x))))


def _rms_mod(x, g, shift, scale):
    ms = jnp.mean(x * x, axis=-1, keepdims=True)
    y = (x * lax.rsqrt(ms + RMS_EPS)) * g
    return y * (1.0 + scale) + shift


def _swiglu(h, wgu_ref, wd_ref, act_ref):
    d_ff = wd_ref.shape[0]
    for lo in range(0, d_ff, FF_CHUNK):
        hi = min(lo + FF_CHUNK, d_ff)
        ug = jnp.dot(h, wgu_ref[:, lo:hi], preferred_element_type=F32)
        uu = jnp.dot(h, wgu_ref[:, d_ff + lo:d_ff + hi], preferred_element_type=F32)
        act_ref[:, lo:hi] = (_silu(ug) * uu).astype(BF16)
    return jnp.dot(act_ref[...], wd_ref[...], preferred_element_type=F32)


WEIGHT_CHUNK_ROWS = 128


def _hbm():
    return pl.BlockSpec(memory_space=pl.ANY)


def _load_weights_bf16(jobs, stage_ref, sem):
    rows = WEIGHT_CHUNK_ROWS
    width = stage_ref.shape[-1]
    groups = []
    for src, dst in jobs:
        cols = dst.shape[1]
        chunks = list(range(dst.shape[0] // rows))
        per_slot = width // cols
        groups += [(src, dst, cols, chunks[k:k + per_slot]) for k in range(0, len(chunks), per_slot)]

    def copies(group, slot):
        src, _, cols, chunks = group
        return [pltpu.make_async_copy(src.at[pl.ds(i * rows, rows), :],
                                      stage_ref.at[slot, :, pl.ds(p * cols, cols)], sem.at[slot])
                for p, i in enumerate(chunks)]

    for cp in copies(groups[0], 0):
        cp.start()
    for k, group in enumerate(groups):
        slot = k % 2
        if k + 1 < len(groups):
            for cp in copies(groups[k + 1], 1 - slot):
                cp.start()
        for cp in copies(group, slot):
            cp.wait()
        _, dst, cols, chunks = group
        for p, i in enumerate(chunks):
            dst[i * rows:(i + 1) * rows, :] = stage_ref[slot, :, p * cols:(p + 1) * cols].astype(BF16)


def _weight_scratch(shapes):
    width = max(s[1] for s in shapes)
    return ([pltpu.VMEM(s, BF16) for s in shapes]
            + [pltpu.VMEM((2, WEIGHT_CHUNK_ROWS, width), F32), pltpu.SemaphoreType.DMA((2,))])


def _mods_kernel(c_ref, w_ref, b_ref, o_ref):
    h = _silu(c_ref[...]).astype(BF16)
    w = w_ref[0].astype(BF16)
    o_ref[0] = jnp.dot(h, w, preferred_element_type=F32) + b_ref[0]


def _mods(c, c_ctx, ada_w, ada_b):
    depth, d, n = ada_w.shape
    b = c.shape[0]
    rows = jnp.zeros((SUBLANES, d), F32).at[:b].set(c).at[b].set(c_ctx)
    tn = n // 4
    out = pl.pallas_call(
        _mods_kernel,
        grid=(depth, n // tn),
        in_specs=[
            pl.BlockSpec((SUBLANES, d), lambda i, j: (0, 0)),
            pl.BlockSpec((1, d, tn), lambda i, j: (i, 0, j)),
            pl.BlockSpec((1, 1, tn), lambda i, j: (i, 0, j)),
        ],
        out_specs=pl.BlockSpec((1, SUBLANES, tn), lambda i, j: (i, 0, j)),
        out_shape=jax.ShapeDtypeStruct((depth, SUBLANES, n), F32),
        compiler_params=_cparams(2, 40),
        name="mods",
    )(rows, ada_w, ada_b.reshape(depth, 1, n))
    return out.reshape(depth, SUBLANES, N_MOD, d)


class _TokenGrid:
    def __init__(self, n_batch, seq, n_ctx, with_ctx=True):
        self.tm = TOKEN_TILE
        self.n_batch = n_batch
        self.per_batch = seq // self.tm
        self.x_tiles = n_batch * self.per_batch
        self.c_tiles = (n_batch * n_ctx) // self.tm if with_ctx else 0
        self.n_x = n_batch * seq
        self.n_tok = self.n_x + (n_batch * n_ctx if with_ctx else 0)

    @property
    def grid(self):
        return (self.x_tiles + self.c_tiles,)

    def _x_tile(self, i):
        return jnp.minimum(i, self.x_tiles - 1)

    def _c_tile(self, i):
        return jnp.maximum(i - self.x_tiles, 0)

    def merged(self, width):
        return pl.BlockSpec((1, self.tm, width), lambda i: (0, i, 0))

    def x_only(self, width):
        return pl.BlockSpec((1, self.tm, width), lambda i: (0, self._x_tile(i), 0))

    def c_only(self, width):
        return pl.BlockSpec((1, self.tm, width), lambda i: (0, self._c_tile(i), 0))

    def x_slab(self, n_slabs):
        def index(i):
            t = self._x_tile(i)
            return (t // self.per_batch, 0, t % self.per_batch, 0)
        return pl.BlockSpec((1, n_slabs, self.tm, LANES), index)

    def c_slab(self, n_slabs):
        return pl.BlockSpec((1, n_slabs, self.tm, LANES), lambda i: (0, 0, self._c_tile(i), 0))

    def mod(self, d):
        return pl.BlockSpec((1, N_MOD, d), lambda i: (jnp.minimum(i // self.per_batch, self.n_batch), 0, 0))

    def is_x(self, step):
        return step < self.x_tiles


def _pre_kernel(*refs, splits, n_src, x_tiles, w_index):
    x_refs, mod_ref = refs[:n_src], refs[n_src]
    g_ref, wgu_hbm, wd_hbm, win_hbm = refs[n_src + 1:n_src + 5]
    pos = n_src + 5
    x_out = refs[pos]
    pos += 1
    out_refs = []
    for _, _, slab in splits:
        out_refs.append(refs[pos:pos + (2 if slab else 1)])
        pos += 2 if slab else 1
    act_ref, wgu_ref, wd_ref, win_ref, stage_ref, sem = refs[pos:]
    step = pl.program_id(0)
    layer, mixer = w_index
    is_x = step < x_tiles

    @pl.when(step == 0)
    def _():
        _load_weights_bf16([(wgu_hbm.at[layer, 0], wgu_ref), (wd_hbm.at[layer, 0], wd_ref),
                            (win_hbm.at[mixer], win_ref)], stage_ref, sem)

    x = x_refs[0][0] if n_src == 1 else jnp.where(is_x, x_refs[0][0], x_refs[1][0])
    mod = mod_ref[0]
    h = _rms_mod(x, g_ref[0:1, :], mod[0:1, :], mod[1:2, :]).astype(BF16)
    y = _swiglu(h, wgu_ref, wd_ref, act_ref)
    x = x + (0.5 * mod[2:3, :]) * y
    x_out[0] = x
    h = _rms_mod(x, g_ref[1:2, :], mod[3:4, :], mod[4:5, :]).astype(BF16)
    off = 0
    for o_refs, (width, scale, slab) in zip(out_refs, splits):
        u = jnp.dot(h, win_ref[:, off:off + width], preferred_element_type=F32)
        if scale != 1.0:
            u = u * scale
        off += width
        if not slab:
            o_refs[0][0] = u.astype(o_refs[0].dtype)
            continue

        def write(o_ref, u=u, width=width):
            for j in range(width // LANES):
                o_ref[0, j] = u[:, j * LANES:(j + 1) * LANES].astype(o_ref.dtype)

        pl.when(is_x)(functools.partial(write, o_refs[0]))
        pl.when(jnp.logical_not(is_x))(functools.partial(write, o_refs[1]))


def _pre(tg, xs, mod, g, wgu, wd, win, w_index, splits, dtypes, name):
    d = xs[0].shape[-1]
    n_src = len(xs)
    in_specs = [tg.merged(d)] if n_src == 1 else [tg.x_only(d), tg.c_only(d)]
    in_specs += [tg.mod(d), _resident(g.shape), _hbm(), _hbm(), _hbm()]
    out_shape = [jax.ShapeDtypeStruct((1, tg.n_tok, d), F32)]
    out_specs = [tg.merged(d)]
    for (w, _, is_slab), dt in zip(splits, dtypes):
        if is_slab:
            k = w // LANES
            out_shape += [jax.ShapeDtypeStruct((tg.n_batch, k, tg.n_x // tg.n_batch, LANES), dt),
                          jax.ShapeDtypeStruct((1, k, tg.n_tok - tg.n_x, LANES), dt)]
            out_specs += [tg.x_slab(k), tg.c_slab(k)]
        else:
            out_shape.append(jax.ShapeDtypeStruct((1, tg.n_tok, w), dt))
            out_specs.append(tg.merged(w))
    w_shapes = [wgu.shape[-2:], wd.shape[-2:], win.shape[-2:]]
    outs = pl.pallas_call(
        functools.partial(_pre_kernel, splits=splits, n_src=n_src, x_tiles=tg.x_tiles, w_index=w_index),
        grid=tg.grid,
        in_specs=in_specs,
        out_specs=out_specs,
        out_shape=out_shape,
        scratch_shapes=[pltpu.VMEM((tg.tm, wd.shape[-2]), BF16)] + _weight_scratch(w_shapes),
        compiler_params=_cparams(1, 56),
        name=name,
    )(*xs, mod, g, wgu, wd, win)
    result, pos = [outs[0]], 1
    for _, _, is_slab in splits:
        result.append(tuple(outs[pos:pos + 2]) if is_slab else outs[pos])
        pos += 2 if is_slab else 1
    return result


def _post_kernel(*refs, n_parts, final, gated_scan, x_tiles, w_index):
    x_ref, mod_ref, g_ref = refs[:3]
    pos = 3
    if gated_scan:
        part_refs = refs[pos:pos + 3]
        pos += 3
    else:
        part_refs = [refs[pos + 2 * j:pos + 2 * j + 2] for j in range(n_parts)]
        pos += 2 * n_parts
    wout_hbm, wgu_hbm, wd_hbm = refs[pos:pos + 3]
    pos += 3
    if final:
        gf_ref = refs[pos]
        pos += 1
    o_ref, act_ref = refs[pos:pos + 2]
    pos += 2
    if gated_scan:
        mix_ref = refs[pos]
        pos += 1
    wout_ref, wgu_ref, wd_ref, stage_ref, sem = refs[pos:]
    step = pl.program_id(0)
    layer, mixer = w_index

    @pl.when(step == 0)
    def _():
        _load_weights_bf16([(wout_hbm.at[mixer], wout_ref), (wgu_hbm.at[layer, 1], wgu_ref),
                            (wd_hbm.at[layer, 1], wd_ref)], stage_ref, sem)

    if gated_scan:
        gate_ref, hf_ref, hb_ref = part_refs
        per_dot = MXU_DIM // LANES
        acc = None
        for j0 in range(0, hf_ref.shape[1], per_dot):
            for j in range(j0, j0 + per_dot):
                gate = gate_ref[0, :, j * LANES:(j + 1) * LANES].astype(F32)
                hsum = hf_ref[0, j].astype(F32) + hb_ref[0, j].astype(F32)
                mix_ref[:, j * LANES:(j + 1) * LANES] = (_gelu_tanh(gate) * hsum).astype(BF16)
            k = slice(j0 * LANES, (j0 + per_dot) * LANES)
            part = jnp.dot(mix_ref[:, k], wout_ref[k, :], preferred_element_type=F32)
            acc = part if acc is None else acc + part
    else:
        is_x = step < x_tiles
        acc = None
        off = 0
        for px_ref, pc_ref in part_refs:
            y = jnp.where(is_x, px_ref[0], pc_ref[0])
            w = y.shape[-1]
            part = jnp.dot(y, wout_ref[off:off + w, :], preferred_element_type=F32)
            acc = part if acc is None else acc + part
            off += w
    mod = mod_ref[0]
    x = x_ref[0] + mod[5:6, :] * acc
    h = _rms_mod(x, g_ref[2:3, :], mod[6:7, :], mod[7:8, :]).astype(BF16)
    y = _swiglu(h, wgu_ref, wd_ref, act_ref)
    x = x + (0.5 * mod[8:9, :]) * y
    if final:
        ms = jnp.mean(x * x, axis=-1, keepdims=True)
        x = (x * lax.rsqrt(ms + RMS_EPS)) * gf_ref[...]
    o_ref[0] = x


def _post(tg, x, mod, g, parts, wout, wgu, wd, w_index, final_g, name, gated_scan=False):
    d = x.shape[-1]
    args = [x, mod, g]
    in_specs = [tg.merged(d), tg.mod(d), _resident(g.shape)]
    if gated_scan:
        gate, hf, hb = parts
        args += [gate, hf, hb]
        in_specs += [tg.merged(gate.shape[-1]), tg.x_slab(hf.shape[1]), tg.x_slab(hb.shape[1])]
    else:
        for px, pc in parts:
            args += [px, pc]
            in_specs += [tg.x_only(px.shape[-1]), tg.c_only(pc.shape[-1])]
    args += [wout, wgu, wd]
    in_specs += [_hbm(), _hbm(), _hbm()]
    if final_g is not None:
        args.append(final_g)
        in_specs.append(_resident(final_g.shape))
    scratch = [pltpu.VMEM((tg.tm, wd.shape[-2]), BF16)]
    if gated_scan:
        scratch.append(pltpu.VMEM((tg.tm, wout.shape[-2]), BF16))
    scratch += _weight_scratch([wout.shape[-2:], wgu.shape[-2:], wd.shape[-2:]])
    return pl.pallas_call(
        functools.partial(_post_kernel, n_parts=len(parts), final=final_g is not None,
                          gated_scan=gated_scan, x_tiles=tg.x_tiles, w_index=w_index),
        grid=tg.grid,
        in_specs=in_specs,
        out_specs=tg.merged(d),
        out_shape=jax.ShapeDtypeStruct((1, tg.n_tok, d), F32),
        scratch_shapes=scratch,
        compiler_params=_cparams(1, 56),
        name=name,
    )(*args)


POOL_PAD = SUBLANES * 2


def _pool_kernel(p_ref, w_ref, s_ref, o_ref, pad_ref):
    t = p_ref.shape[1]
    gw = w_ref.shape[-1]
    zeros = jnp.zeros((POOL_PAD, gw), F32)
    pad_ref[0:POOL_PAD, :] = zeros

    def count(row0, n_rows, half):
        pos = row0 + lax.broadcasted_iota(jnp.int32, (n_rows, gw), 0)
        return (jnp.minimum(pos + half, t) - jnp.maximum(pos - half, 0)).astype(F32)

    for gi, win in enumerate(POOL_WINDOWS):
        half = win // 2
        u = p_ref[0, :, gi * gw:(gi + 1) * gw].astype(F32)
        ext = t + POOL_PAD
        acc = jnp.concatenate([u, zeros], axis=0)
        span = 1
        while span < win:
            pad_ref[POOL_PAD:POOL_PAD + ext, :] = acc
            acc = acc + pad_ref[POOL_PAD - span:POOL_PAD - span + ext, :]
            span *= 2
        pad_ref[POOL_PAD:POOL_PAD + ext, :] = acc
        wsum = pad_ref[POOL_PAD + half - 1:POOL_PAD + half - 1 + t, :]
        if win & (win - 1) == 0 and half <= SUBLANES <= t // 2:
            head, body, tail = slice(0, SUBLANES), slice(SUBLANES, t - SUBLANES), slice(t - SUBLANES, t)
            y = jnp.concatenate([wsum[head] / count(0, SUBLANES, half) - u[head],
                                 wsum[body] * (1.0 / win) - u[body],
                                 wsum[tail] / count(t - SUBLANES, SUBLANES, half) - u[tail]], axis=0)
        else:
            y = wsum / count(0, t, half) - u
        y = y.astype(BF16)
        z = jnp.dot(y, w_ref[gi], preferred_element_type=F32)
        o_ref[0, :, gi * gw:(gi + 1) * gw] = (z * s_ref[:, gi * gw:(gi + 1) * gw]).astype(o_ref.dtype)


def _pool(p, pool_w, pool_scale, n, t, first_block, name):
    w = p.shape[-1]
    gw = pool_w.shape[-1]
    return pl.pallas_call(
        _pool_kernel,
        grid=(n,),
        in_specs=[
            pl.BlockSpec((1, t, w), lambda b: (0, first_block + b, 0)),
            _resident(pool_w.shape), _resident(pool_scale.shape),
        ],
        out_specs=pl.BlockSpec((1, t, w), lambda b: (0, b, 0)),
        out_shape=jax.ShapeDtypeStruct((1, n * t, w), BF16),
        scratch_shapes=[pltpu.VMEM((t + 2 * POOL_PAD, gw), F32)],
        compiler_params=_cparams(1, 48),
        name=name,
    )(p, pool_w, pool_scale)


def _head_block_mask(shape):
    r = lax.broadcasted_iota(jnp.int32, shape, 0) // NA_HEAD_DIM
    c = lax.broadcasted_iota(jnp.int32, shape, 1) // NA_HEAD_DIM
    return r == c


def _attend(q_blk, key_parts, val_parts, bias_parts):
    nq, g = q_blk.shape
    heads = g // NA_HEAD_DIM
    qbd = jnp.where(_head_block_mask((heads * nq, g)),
                    jnp.concatenate([q_blk] * heads, axis=0), jnp.zeros((), BF16))
    nt = (((1,), (1,)), ((), ()))
    scores = []
    for keys, bias in zip(key_parts, bias_parts):
        s = lax.dot_general(keys, qbd, nt, preferred_element_type=F32)
        scores.append(s if bias is None else s + bias)
    m = functools.reduce(jnp.maximum, [jnp.max(s, axis=0, keepdims=True) for s in scores])
    ps = [jnp.exp2(s - m) for s in scores]
    l = functools.reduce(jnp.add, [jnp.sum(p, axis=0, keepdims=True) for p in ps])
    tn = (((0,), (0,)), ((), ()))
    r = None
    for p, vals in zip(ps, val_parts):
        part = lax.dot_general(p.astype(BF16), vals, tn, preferred_element_type=F32)
        r = part if r is None else r + part
    inv_t = jnp.transpose(jnp.broadcast_to(1.0 / l, (LANES, heads * nq)))
    inv_rows = jnp.concatenate([inv_t] * (g // LANES), axis=1)
    lane_head = lax.broadcasted_iota(jnp.int32, (nq, g), 1) // NA_HEAD_DIM
    out = jnp.zeros((nq, g), F32)
    for h in range(heads):
        rows = slice(h * nq, (h + 1) * nq)
        out = jnp.where(lane_head == h, r[rows, :] * inv_rows[rows, :], out)
    return out


NATTN_UNROLL = 32


def _nattn_kernel(q_ref, k_ref, v_ref, qc_ref, kc_ref, vc_ref, bias_ref, o_ref, oc_ref):
    s = q_ref.shape[1]
    rows = s // GRID_W
    kh = min(NB_ROWS, rows)
    kc = kc_ref[0]
    vc = vc_ref[0]

    def body(i, carry):
        r0 = jnp.clip(i - kh // 2, 0, rows - kh)
        q0 = pl.multiple_of(i * GRID_W, GRID_W)
        k0 = pl.multiple_of(r0 * GRID_W, GRID_W)
        q_blk = q_ref[0, pl.ds(q0, GRID_W), :]
        keys = k_ref[0, pl.ds(k0, kh * GRID_W), :]
        vals = v_ref[0, pl.ds(k0, kh * GRID_W), :]
        bias = bias_ref[0, pl.ds(r0 - i + (NB_ROWS - 1), kh), :, :]
        bias = bias.reshape(kh * GRID_W, bias.shape[-1])
        out = _attend(q_blk, [keys, kc], [vals, vc], [bias, None])
        o_ref[0, pl.ds(q0, GRID_W), :] = out.astype(o_ref.dtype)
        return carry

    lax.fori_loop(0, rows, body, 0, unroll=NATTN_UNROLL)

    for i in range(qc_ref.shape[1] // GRID_W):
        blk = slice(i * GRID_W, (i + 1) * GRID_W)
        oc_ref[0, blk, :] = _attend(qc_ref[0, blk, :], [kc], [vc], [None]).astype(oc_ref.dtype)


def _bias_table(rpb):
    h = rpb.shape[0]
    kcol = np.arange(GRID_W)[:, None]
    qcol = np.arange(GRID_W)[None, :]
    qstart = np.clip(qcol - NB_COLS // 2, 0, GRID_W - NB_COLS)
    mask = (kcol >= qstart) & (kcol < qstart + NB_COLS)
    idx = np.clip(kcol - qcol + NB_COLS - 1, 0, 2 * NB_COLS - 2)
    onehot = (idx[None] == np.arange(rpb.shape[-1])[:, None, None]).astype(np.float32)
    t = jnp.einsum("hrj,jkq->hrkq", rpb.astype(F32), onehot, precision=lax.Precision.HIGHEST)
    t = jnp.where(mask[None, None], t * LOG2_E, -jnp.inf)
    t = t.reshape(h // HEADS_PER_GROUP, HEADS_PER_GROUP, t.shape[1], GRID_W, GRID_W)
    t = jnp.transpose(t, (0, 2, 3, 1, 4))
    return t.reshape(h // HEADS_PER_GROUP, t.shape[1], GRID_W, HEADS_PER_GROUP * GRID_W)


def _nattn(q, k, v, rpb, b, s, n_ctx):
    w = q.shape[-1]
    g = HEADS_PER_GROUP * NA_HEAD_DIM
    bias = _bias_table(rpb)
    ctx0 = (b * s) // n_ctx
    lat = pl.BlockSpec((1, s, g), lambda i, j: (0, i, j))
    ctx = pl.BlockSpec((1, n_ctx, g), lambda i, j: (0, ctx0 + i, j))
    return pl.pallas_call(
        _nattn_kernel,
        grid=(b, w // g),
        in_specs=[lat, lat, lat, ctx, ctx, ctx,
                  pl.BlockSpec((1,) + bias.shape[1:], lambda i, j: (j, 0, 0, 0))],
        out_specs=[lat, pl.BlockSpec((1, n_ctx, g), lambda i, j: (0, i, j))],
        out_shape=[jax.ShapeDtypeStruct((1, b * s, w), BF16),
                   jax.ShapeDtypeStruct((1, b * n_ctx, w), BF16)],
        compiler_params=_cparams(2, 48),
        name="nattn",
    )(q, k, v, q, k, v, bias)


RG_CHUNK = 128
HALO = SUBLANES
SCAN_UNROLL = 16
COEF_UNROLL = 2


def _rglru_kernel(*refs, n_ctx, latent):
    if latent:
        (rxf_ref, rxf_prev_ref, rxf_next_ref, rxb_ref, rxb_prev_ref, rxb_next_ref, init_ref,
         cw_ref, cb_ref, gw_ref, lam_ref, hf_ref, hb_ref,
         win_ref, af_ref, bf_ref, ab_ref, bb_ref, sf_ref, sb_ref, state_ref) = refs
        nbatch, nblk, tc, _ = rxf_ref.shape
    else:
        (rc_ref, cw_ref, cb_ref, gw_ref, lam_ref, final_ref,
         win_ref, af_ref, bf_ref, ab_ref, bb_ref, state_ref) = refs
        nblk, tc = rc_ref.shape[0], RG_CHUNK
        nbatch = rc_ref.shape[1] // n_ctx
    ngroups = nblk // COEF_UNROLL
    n_chunks = pl.num_programs(0)
    c = pl.program_id(0)

    @pl.when(c == 0)
    def _():
        state_ref[...] = init_ref[...] if latent else jnp.zeros(state_ref.shape, F32)

    lane = lax.broadcasted_iota(jnp.int32, (nbatch * tc, RG_BW), 1)
    bias_lanes = (lane < 2).astype(F32).astype(BF16)

    def fill_from_ctx(win, n, j):
        for b in range(nbatch):
            base = b * n_ctx + j * tc
            prev0 = pl.multiple_of(jnp.maximum(base - HALO, b * n_ctx), HALO)
            next0 = pl.multiple_of(jnp.minimum(base + tc, (b + 1) * n_ctx - HALO), HALO)
            win[b, 0:HALO, :] = jnp.where(j > 0, rc_ref[n, pl.ds(prev0, HALO), :], 0.0)
            win[b, HALO:HALO + tc, :] = rc_ref[n, pl.ds(pl.multiple_of(base, tc), tc), :]
            win[b, HALO + tc:2 * HALO + tc, :] = jnp.where(
                j < n_chunks - 1, rc_ref[n, pl.ds(next0, HALO), :], 0.0)

    def fill_from_x(win, n, j, cur_ref, prev_ref, next_ref):
        for b in range(nbatch):
            win[b, 0:HALO, :] = jnp.where(j > 0, prev_ref[b, n], 0.0)
            win[b, HALO:HALO + tc, :] = cur_ref[b, n]
            win[b, HALO + tc:2 * HALO + tc, :] = jnp.where(j < n_chunks - 1, next_ref[b, n], 0.0)

    def coefficients(d, j, x_refs, a_ref, b_ref):
        cols = slice(d * 2 * RG_BW, (d + 1) * 2 * RG_BW)

        def fill(n, win):
            if latent:
                fill_from_x(win, n, j, *x_refs)
            else:
                fill_from_ctx(win, n, j)

        def body(g, p, win):
            n = g * COEF_UNROLL + p
            cw = cw_ref[n]
            hus = []
            for b in range(nbatch):
                y = cb_ref[n]
                for k in range(CONV_W):
                    lo = HALO + k - CONV_W // 2
                    y = y + win[b, lo:lo + tc, :] * cw[k:k + 1, :]
                hus.append(y)
            hu = jnp.concatenate(hus, axis=0)
            lhs = jnp.concatenate([hu.astype(BF16), bias_lanes], axis=1)
            z = jnp.dot(lhs, gw_ref[n, :, cols], preferred_element_type=F32)
            t_r = jnp.tanh(z[:, :RG_BW])
            t_i = jnp.tanh(z[:, RG_BW:])
            lam = -lam_ref[n, d:d + 1, :]
            softplus = jnp.maximum(lam, 0.0) + jnp.log1p(jnp.exp(-jnp.abs(lam)))
            log_a = (t_r + 1.0) * ((-0.5 * RG_C) * softplus)
            a = jnp.exp(log_a)
            th = jnp.tanh(log_a)
            num = -2.0 * th
            root = jnp.where(num > 0.0, num * lax.rsqrt(num * (1.0 - th)), 0.0)
            coef = root * ((t_i + 1.0) * hu)
            for b in range(nbatch):
                rows = pl.ds(p * nbatch + b, tc, stride=SUBLANES)
                a_ref[g, rows, :] = a[b * tc:(b + 1) * tc]
                b_ref[g, rows, :] = coef[b * tc:(b + 1) * tc]

        return fill, body

    x_refs_f = (rxf_ref, rxf_prev_ref, rxf_next_ref) if latent else None
    x_refs_b = (rxb_ref, rxb_prev_ref, rxb_next_ref) if latent else None
    stages = [coefficients(0, c, x_refs_f, af_ref, bf_ref),
              coefficients(1, n_chunks - 1 - c, x_refs_b, ab_ref, bb_ref)]

    def group(g, carry):
        for d, (fill, _) in enumerate(stages):
            for p in range(COEF_UNROLL):
                fill(g * COEF_UNROLL + p, win_ref.at[d * COEF_UNROLL + p])
        for d, (_, body) in enumerate(stages):
            for p in range(COEF_UNROLL):
                body(g, p, win_ref.at[d * COEF_UNROLL + p])
        return carry

    lax.fori_loop(0, ngroups, group, 0, unroll=True)

    def scan_step(t, hs):
        out_f, out_b = [], []
        rows_f = pl.ds(pl.multiple_of(t * SUBLANES, SUBLANES), SUBLANES)
        rows_b = pl.ds(pl.multiple_of((tc - 1 - t) * SUBLANES, SUBLANES), SUBLANES)
        for g in range(ngroups):
            h = af_ref[g, rows_f, :] * hs[g] + bf_ref[g, rows_f, :]
            if latent:
                sf_ref[g, rows_f, :] = h
            out_f.append(h)
            h = ab_ref[g, rows_b, :] * hs[ngroups + g] + bb_ref[g, rows_b, :]
            if latent:
                sb_ref[g, rows_b, :] = h
            out_b.append(h)
        return tuple(out_f + out_b)

    hs = tuple(state_ref[d, g] for d in range(2) for g in range(ngroups))
    hs = lax.fori_loop(0, tc, scan_step, hs, unroll=SCAN_UNROLL)
    for d in range(2):
        for g in range(ngroups):
            state_ref[d, g] = hs[d * ngroups + g]

    if not latent:
        final_ref[...] = state_ref[...]
        return

    def emit(g, carry):
        for p in range(COEF_UNROLL):
            for b in range(nbatch):
                rows = pl.ds(p * nbatch + b, tc, stride=SUBLANES)
                hf_ref[b, g * COEF_UNROLL + p] = sf_ref[g, rows, :].astype(hf_ref.dtype)
                hb_ref[b, g * COEF_UNROLL + p] = sb_ref[g, rows, :].astype(hb_ref.dtype)
        return carry
    lax.fori_loop(0, ngroups, emit, 0, unroll=True)


def _rglru(rx, rc, n_ctx, conv_w, conv_b, gate_w, gate_b, lam):
    b, nb, s, _ = rx.shape
    tc = RG_CHUNK
    ctx_chunks = n_ctx // tc
    x_chunks = s // tc
    per_halo = tc // HALO
    gw = jnp.transpose(gate_w, (2, 3, 0, 1, 4)).reshape(nb, RG_BW, 4 * RG_BW).astype(BF16)
    gb = 0.5 * jnp.transpose(gate_b.reshape(2, 2, nb, RG_BW), (2, 0, 1, 3)).reshape(nb, 1, 4 * RG_BW)
    gb_hi = gb.astype(BF16)
    gb_lo = (gb - gb_hi.astype(F32)).astype(BF16)
    pad = jnp.zeros((nb, MXU_DIM - RG_BW - 2, 4 * RG_BW), BF16)
    gw = jnp.concatenate([gw, gb_hi, gb_lo, pad], axis=1)
    cw = 0.5 * jnp.transpose(conv_w.reshape(CONV_W, nb, RG_BW), (1, 0, 2))
    cb = 0.5 * conv_b.reshape(nb, 1, RG_BW)
    lam_s = jnp.transpose(lam.reshape(2, nb, RG_BW), (1, 0, 2))

    assert COEF_UNROLL * b == SUBLANES and nb % COEF_UNROLL == 0, (b, nb)
    ngroups = nb // COEF_UNROLL
    slab = pltpu.VMEM((ngroups, tc * SUBLANES, LANES), F32)
    window = pltpu.VMEM((2 * COEF_UNROLL, b, tc + 2 * HALO, LANES), F32)
    carry = jax.ShapeDtypeStruct((2, ngroups, SUBLANES, LANES), F32)
    carry_scratch = pltpu.VMEM(carry.shape, F32)
    params = [cw, cb, gw, lam_s]
    param_specs = [_resident(a.shape) for a in params]

    ctx_state = pl.pallas_call(
        functools.partial(_rglru_kernel, n_ctx=n_ctx, latent=False),
        grid=(ctx_chunks,),
        in_specs=[_resident(rc.shape)] + param_specs,
        out_specs=pl.BlockSpec(carry.shape, lambda c: (0, 0, 0, 0)),
        out_shape=carry,
        scratch_shapes=[window, slab, slab, slab, slab, carry_scratch],
        compiler_params=_cparams(1, 32),
        name="rglru_ctx",
    )(rc, *params)

    chunk_f = lambda c: c
    chunk_b = lambda c: x_chunks - 1 - c
    cur = lambda f: pl.BlockSpec((b, nb, tc, LANES), lambda c: (0, 0, f(c), 0))
    prev = lambda f: pl.BlockSpec((b, nb, HALO, LANES),
                                  lambda c: (0, 0, jnp.maximum(f(c) * per_halo - 1, 0), 0))
    nxt = lambda f: pl.BlockSpec((b, nb, HALO, LANES),
                                 lambda c: (0, 0, jnp.minimum((f(c) + 1) * per_halo, s // HALO - 1), 0))
    state_shape = jax.ShapeDtypeStruct((b, nb, s, LANES), BF16)
    return pl.pallas_call(
        functools.partial(_rglru_kernel, n_ctx=n_ctx, latent=True),
        grid=(x_chunks,),
        in_specs=[cur(chunk_f), prev(chunk_f), nxt(chunk_f), cur(chunk_b), prev(chunk_b), nxt(chunk_b),
                  _resident(carry.shape)] + param_specs,
        out_specs=[cur(chunk_f), cur(chunk_b)],
        out_shape=[state_shape, state_shape],
        scratch_shapes=[window, slab, slab, slab, slab, slab, slab, carry_scratch],
        compiler_params=_cparams(1, 52),
        name="rglru",
    )(rx, rx, rx, rx, rx, rx, ctx_state, *params)


def kernel(x, c, ctx, c_ctx, ada_w, ada_b, norm_g, ffn_w_gu, ffn_w_down, ab_w_in, ab_w_out, pool_w,
           pool_scale, na_rpb, rg_w_in, rg_conv_w, rg_conv_b, rg_gate_w, rg_gate_b, rg_lambda,
           rg_w_out, final_g):
    b, s, d = x.shape
    n_ctx = ctx.shape[1]
    depth = ada_w.shape[0]
    pool_width = pool_w.shape[1] * pool_w.shape[2]
    na_width = (ab_w_in.shape[-1] - pool_width) // 3
    d_rnn = rg_w_out.shape[1]
    na_scale = NA_HEAD_DIM ** -0.5 * LOG2_E

    mods = _mods(c, c_ctx, ada_w, ada_b)
    tg = _TokenGrid(b, s, n_ctx)
    tg_x = _TokenGrid(b, s, n_ctx, with_ctx=False)
    xs = [x.reshape(1, b * s, d), ctx.reshape(1, b * n_ctx, d)]

    for i in range(depth):
        last = i == depth - 1
        mod = mods[i, :b + 1]
        g = norm_g[i]
        j = i // 2
        fg = final_g.reshape(1, d) if last else None
        if i % 2 == 0:
            pw = pool_w[j].astype(BF16)
            ps = pool_scale[j].reshape(1, pool_width)
            splits = ((pool_width, 1.0, False), (na_width, na_scale, False), (na_width, 1.0, False),
                      (na_width, 1.0, False))
            dts = (BF16, BF16, BF16, BF16)
            xm, p, q, k, v = _pre(tg, xs, mod, g, ffn_w_gu, ffn_w_down, ab_w_in, (i, j), splits, dts,
                                  f"pre{i}")
            a_x = _pool(p, pw, ps, b, s, 0, f"pool{i}_x")
            b_x, b_c = _nattn(q, k, v, na_rpb[j], b, s, n_ctx)
            if last:
                raise NotImplementedError("a final pooling/attention layer is not needed at this depth")
            a_c = _pool(p, pw, ps, b, n_ctx, (b * s) // n_ctx, f"pool{i}_c")
            xm = _post(tg, xm, mod, g, [(a_x, a_c), (b_x, b_c)], ab_w_out, ffn_w_gu, ffn_w_down, (i, j),
                       fg, f"post{i}")
        else:
            splits = ((d_rnn, 1.0, False), (d_rnn, 1.0, True))
            dts = (BF16, F32)
            xm, gate, (rx, rc) = _pre(tg, xs, mod, g, ffn_w_gu, ffn_w_down, rg_w_in, (i, j), splits, dts,
                                      f"pre{i}")
            hf, hb = _rglru(rx, rc[0], n_ctx, rg_conv_w[j], rg_conv_b[j], rg_gate_w[j], rg_gate_b[j],
                            rg_lambda[j])
            if not last:
                raise NotImplementedError("context output of an RG-LRU layer is not needed at this depth")
            xm = _post(tg_x, xm, mod, g, (gate, hf, hb), rg_w_out, ffn_w_gu, ffn_w_down, (i, j), fg,
                       f"post{i}", gated_scan=True)
        xs = [xm]
    return xm.reshape(b, s, d)
```

```python
import functools

import numpy as np
import jax
import jax.numpy as jnp
from jax import lax
from jax.experimental import pallas as pl
from jax.experimental.pallas import tpu as pltpu

F32 = jnp.float32
BF16 = jnp.bfloat16

N_MOD = 9
RMS_EPS = 1e-6
GRID_W = 64
POOL_WINDOWS = (2, 4, 8, 16)
NA_HEAD_DIM = 64
NB_ROWS = 8
NB_COLS = 16
RG_BW = 128
CONV_W = 4
RG_C = 8.0
LOG2_E = float(np.log2(np.e))

LANES = 128
SUBLANES = 8
MXU_DIM = 256
VMEM_BYTES = 64 * 1024 * 1024

HEADS_PER_GROUP = MXU_DIM // NA_HEAD_DIM
TOKEN_TILE = 512
FF_CHUNK = MXU_DIM


def _cparams(n_axes, vmem_mb):
    return pltpu.CompilerParams(
        dimension_semantics=("arbitrary",) * n_axes,
        vmem_limit_bytes=min(vmem_mb * 1024 * 1024, VMEM_BYTES - 8 * 1024 * 1024),
    )


def _resident(shape):
    nd = len(shape)
    return pl.BlockSpec(shape, lambda *_: (0,) * nd, pipeline_mode=pl.Buffered(1))


def _silu(x):
    return x * jax.nn.sigmoid(x)


def _gelu_tanh(x):
    c = np.sqrt(2.0 / np.pi).astype(np.float32)
    return 0.5 * x * (1.0 + jnp.tanh(c * (x + 0.044715 * (x * x * x))))


def _rms_mod(x, g, shift, scale):
    ms = jnp.mean(x * x, axis=-1, keepdims=True)
    y = (x * lax.rsqrt(ms + RMS_EPS)) * g
    return y * (1.0 + scale) + shift


def _swiglu(h, wgu_ref, wd_ref, act_ref):
    d_ff = wd_ref.shape[0]
    for lo in range(0, d_ff, FF_CHUNK):
        hi = min(lo + FF_CHUNK, d_ff)
        ug = jnp.dot(h, wgu_ref[:, lo:hi], preferred_element_type=F32)
        uu = jnp.dot(h, wgu_ref[:, d_ff + lo:d_ff + hi], preferred_element_type=F32)
        act_ref[:, lo:hi] = (_silu(ug) * uu).astype(BF16)
    return jnp.dot(act_ref[...], wd_ref[...], preferred_element_type=F32)


WEIGHT_CHUNK_ROWS = 128


def _hbm():
    return pl.BlockSpec(memory_space=pl.ANY)


def _load_weights_bf16(jobs, stage_ref, sem):
    rows = WEIGHT_CHUNK_ROWS
    width = stage_ref.shape[-1]
    groups = []
    for src, dst in jobs:
        cols = dst.shape[1]
        chunks = list(range(dst.shape[0] // rows))
        per_slot = width // cols
        groups += [(src, dst, cols, chunks[k:k + per_slot]) for k in range(0, len(chunks), per_slot)]

    def copies(group, slot):
        src, _, cols, chunks = group
        return [pltpu.make_async_copy(src.at[pl.ds(i * rows, rows), :],
                                      stage_ref.at[slot, :, pl.ds(p * cols, cols)], sem.at[slot])
                for p, i in enumerate(chunks)]

    for cp in copies(groups[0], 0):
        cp.start()
    for k, group in enumerate(groups):
        slot = k % 2
        if k + 1 < len(groups):
            for cp in copies(groups[k + 1], 1 - slot):
                cp.start()
        for cp in copies(group, slot):
            cp.wait()
        _, dst, cols, chunks = group
        for p, i in enumerate(chunks):
            dst[i * rows:(i + 1) * rows, :] = stage_ref[slot, :, p * cols:(p + 1) * cols].astype(BF16)


def _weight_scratch(shapes):
    width = max(s[1] for s in shapes)
    return ([pltpu.VMEM(s, BF16) for s in shapes]
            + [pltpu.VMEM((2, WEIGHT_CHUNK_ROWS, width), F32), pltpu.SemaphoreType.DMA((2,))])


def _mods_kernel(c_ref, w_ref, b_ref, o_ref):
    h = _silu(c_ref[...]).astype(BF16)
    w = w_ref[0].astype(BF16)
    o_ref[0] = jnp.dot(h, w, preferred_element_type=F32) + b_ref[0]


def _mods(c, c_ctx, ada_w, ada_b):
    depth, d, n = ada_w.shape
    b = c.shape[0]
    rows = jnp.zeros((SUBLANES, d), F32).at[:b].set(c).at[b].set(c_ctx)
    tn = n // 4
    out = pl.pallas_call(
        _mods_kernel,
        grid=(depth, n // tn),
        in_specs=[
            pl.BlockSpec((SUBLANES, d), lambda i, j: (0, 0)),
            pl.BlockSpec((1, d, tn), lambda i, j: (i, 0, j)),
            pl.BlockSpec((1, 1, tn), lambda i, j: (i, 0, j)),
        ],
        out_specs=pl.BlockSpec((1, SUBLANES, tn), lambda i, j: (i, 0, j)),
        out_shape=jax.ShapeDtypeStruct((depth, SUBLANES, n), F32),
        compiler_params=_cparams(2, 40),
        name="mods",
    )(rows, ada_w, ada_b.reshape(depth, 1, n))
    return out.reshape(depth, SUBLANES, N_MOD, d)


class _TokenGrid:
    def __init__(self, n_batch, seq, n_ctx, with_ctx=True):
        self.tm = TOKEN_TILE
        self.n_batch = n_batch
        self.per_batch = seq // self.tm
        self.x_tiles = n_batch * self.per_batch
        self.c_tiles = (n_batch * n_ctx) // self.tm if with_ctx else 0
        self.n_x = n_batch * seq
        self.n_tok = self.n_x + (n_batch * n_ctx if with_ctx else 0)

    @property
    def grid(self):
        return (self.x_tiles + self.c_tiles,)

    def _x_tile(self, i):
        return jnp.minimum(i, self.x_tiles - 1)

    def _c_tile(self, i):
        return jnp.maximum(i - self.x_tiles, 0)

    def merged(self, width):
        return pl.BlockSpec((1, self.tm, width), lambda i: (0, i, 0))

    def x_only(self, width):
        return pl.BlockSpec((1, self.tm, width), lambda i: (0, self._x_tile(i), 0))

    def c_only(self, width):
        return pl.BlockSpec((1, self.tm, width), lambda i: (0, self._c_tile(i), 0))

    def x_slab(self, n_slabs):
        def index(i):
            t = self._x_tile(i)
            return (t // self.per_batch, 0, t % self.per_batch, 0)
        return pl.BlockSpec((1, n_slabs, self.tm, LANES), index)

    def c_slab(self, n_slabs):
        return pl.BlockSpec((1, n_slabs, self.tm, LANES), lambda i: (0, 0, self._c_tile(i), 0))

    def mod(self, d):
        return pl.BlockSpec((1, N_MOD, d), lambda i: (jnp.minimum(i // self.per_batch, self.n_batch), 0, 0))

    def is_x(self, step):
        return step < self.x_tiles


def _pre_kernel(*refs, splits, n_src, x_tiles, w_index):
    x_refs, mod_ref = refs[:n_src], refs[n_src]
    g_ref, wgu_hbm, wd_hbm, win_hbm = refs[n_src + 1:n_src + 5]
    pos = n_src + 5
    x_out = refs[pos]
    pos += 1
    out_refs = []
    for _, _, slab in splits:
        out_refs.append(refs[pos:pos + (2 if slab else 1)])
        pos += 2 if slab else 1
    act_ref, wgu_ref, wd_ref, win_ref, stage_ref, sem = refs[pos:]
    step = pl.program_id(0)
    layer, mixer = w_index
    is_x = step < x_tiles

    @pl.when(step == 0)
    def _():
        _load_weights_bf16([(wgu_hbm.at[layer, 0], wgu_ref), (wd_hbm.at[layer, 0], wd_ref),
                            (win_hbm.at[mixer], win_ref)], stage_ref, sem)

    x = x_refs[0][0] if n_src == 1 else jnp.where(is_x, x_refs[0][0], x_refs[1][0])
    mod = mod_ref[0]
    h = _rms_mod(x, g_ref[0:1, :], mod[0:1, :], mod[1:2, :]).astype(BF16)
    y = _swiglu(h, wgu_ref, wd_ref, act_ref)
    x = x + (0.5 * mod[2:3, :]) * y
    x_out[0] = x
    h = _rms_mod(x, g_ref[1:2, :], mod[3:4, :], mod[4:5, :]).astype(BF16)
    off = 0
    for o_refs, (width, scale, slab) in zip(out_refs, splits):
        u = jnp.dot(h, win_ref[:, off:off + width], preferred_element_type=F32)
        if scale != 1.0:
            u = u * scale
        off += width
        if not slab:
            o_refs[0][0] = u.astype(o_refs[0].dtype)
            continue

        def write(o_ref, u=u, width=width):
            for j in range(width // LANES):
                o_ref[0, j] = u[:, j * LANES:(j + 1) * LANES].astype(o_ref.dtype)

        pl.when(is_x)(functools.partial(write, o_refs[0]))
        pl.when(jnp.logical_not(is_x))(functools.partial(write, o_refs[1]))


def _pre(tg, xs, mod, g, wgu, wd, win, w_index, splits, dtypes, name):
    d = xs[0].shape[-1]
    n_src = len(xs)
    in_specs = [tg.merged(d)] if n_src == 1 else [tg.x_only(d), tg.c_only(d)]
    in_specs += [tg.mod(d), _resident(g.shape), _hbm(), _hbm(), _hbm()]
    out_shape = [jax.ShapeDtypeStruct((1, tg.n_tok, d), F32)]
    out_specs = [tg.merged(d)]
    for (w, _, is_slab), dt in zip(splits, dtypes):
        if is_slab:
            k = w // LANES
            out_shape += [jax.ShapeDtypeStruct((tg.n_batch, k, tg.n_x // tg.n_batch, LANES), dt),
                          jax.ShapeDtypeStruct((1, k, tg.n_tok - tg.n_x, LANES), dt)]
            out_specs += [tg.x_slab(k), tg.c_slab(k)]
        else:
            out_shape.append(jax.ShapeDtypeStruct((1, tg.n_tok, w), dt))
            out_specs.append(tg.merged(w))
    w_shapes = [wgu.shape[-2:], wd.shape[-2:], win.shape[-2:]]
    outs = pl.pallas_call(
        functools.partial(_pre_kernel, splits=splits, n_src=n_src, x_tiles=tg.x_tiles, w_index=w_index),
        grid=tg.grid,
        in_specs=in_specs,
        out_specs=out_specs,
        out_shape=out_shape,
        scratch_shapes=[pltpu.VMEM((tg.tm, wd.shape[-2]), BF16)] + _weight_scratch(w_shapes),
        compiler_params=_cparams(1, 56),
        name=name,
    )(*xs, mod, g, wgu, wd, win)
    result, pos = [outs[0]], 1
    for _, _, is_slab in splits:
        result.append(tuple(outs[pos:pos + 2]) if is_slab else outs[pos])
        pos += 2 if is_slab else 1
    return result


def _post_kernel(*refs, n_parts, final, gated_scan, x_tiles, w_index):
    x_ref, mod_ref, g_ref = refs[:3]
    pos = 3
    if gated_scan:
        part_refs = refs[pos:pos + 3]
        pos += 3
    else:
        part_refs = [refs[pos + 2 * j:pos + 2 * j + 2] for j in range(n_parts)]
        pos += 2 * n_parts
    wout_hbm, wgu_hbm, wd_hbm = refs[pos:pos + 3]
    pos += 3
    if final:
        gf_ref = refs[pos]
        pos += 1
    o_ref, act_ref = refs[pos:pos + 2]
    pos += 2
    if gated_scan:
        mix_ref = refs[pos]
        pos += 1
    wout_ref, wgu_ref, wd_ref, stage_ref, sem = refs[pos:]
    step = pl.program_id(0)
    layer, mixer = w_index

    @pl.when(step == 0)
    def _():
        _load_weights_bf16([(wout_hbm.at[mixer], wout_ref), (wgu_hbm.at[layer, 1], wgu_ref),
                            (wd_hbm.at[layer, 1], wd_ref)], stage_ref, sem)

    if gated_scan:
        gate_ref, hf_ref, hb_ref = part_refs
        per_dot = MXU_DIM // LANES
        acc = None
        for j0 in range(0, hf_ref.shape[1], per_dot):
            for j in range(j0, j0 + per_dot):
                gate = gate_ref[0, :, j * LANES:(j + 1) * LANES].astype(F32)
                hsum = hf_ref[0, j].astype(F32) + hb_ref[0, j].astype(F32)
                mix_ref[:, j * LANES:(j + 1) * LANES] = (_gelu_tanh(gate) * hsum).astype(BF16)
            k = slice(j0 * LANES, (j0 + per_dot) * LANES)
            part = jnp.dot(mix_ref[:, k], wout_ref[k, :], preferred_element_type=F32)
            acc = part if acc is None else acc + part
    else:
        is_x = step < x_tiles
        acc = None
        off = 0
        for px_ref, pc_ref in part_refs:
            y = jnp.where(is_x, px_ref[0], pc_ref[0])
            w = y.shape[-1]
            part = jnp.dot(y, wout_ref[off:off + w, :], preferred_element_type=F32)
            acc = part if acc is None else acc + part
            off += w
    mod = mod_ref[0]
    x = x_ref[0] + mod[5:6, :] * acc
    h = _rms_mod(x, g_ref[2:3, :], mod[6:7, :], mod[7:8, :]).astype(BF16)
    y = _swiglu(h, wgu_ref, wd_ref, act_ref)
    x = x + (0.5 * mod[8:9, :]) * y
    if final:
        ms = jnp.mean(x * x, axis=-1, keepdims=True)
        x = (x * lax.rsqrt(ms + RMS_EPS)) * gf_ref[...]
    o_ref[0] = x


def _post(tg, x, mod, g, parts, wout, wgu, wd, w_index, final_g, name, gated_scan=False):
    d = x.shape[-1]
    args = [x, mod, g]
    in_specs = [tg.merged(d), tg.mod(d), _resident(g.shape)]
    if gated_scan:
        gate, hf, hb = parts
        args += [gate, hf, hb]
        in_specs += [tg.merged(gate.shape[-1]), tg.x_slab(hf.shape[1]), tg.x_slab(hb.shape[1])]
    else:
        for px, pc in parts:
            args += [px, pc]
            in_specs += [tg.x_only(px.shape[-1]), tg.c_only(pc.shape[-1])]
    args += [wout, wgu, wd]
    in_specs += [_hbm(), _hbm(), _hbm()]
    if final_g is not None:
        args.append(final_g)
        in_specs.append(_resident(final_g.shape))
    scratch = [pltpu.VMEM((tg.tm, wd.shape[-2]), BF16)]
    if gated_scan:
        scratch.append(pltpu.VMEM((tg.tm, wout.shape[-2]), BF16))
    scratch += _weight_scratch([wout.shape[-2:], wgu.shape[-2:], wd.shape[-2:]])
    return pl.pallas_call(
        functools.partial(_post_kernel, n_parts=len(parts), final=final_g is not None,
                          gated_scan=gated_scan, x_tiles=tg.x_tiles, w_index=w_index),
        grid=tg.grid,
        in_specs=in_specs,
        out_specs=tg.merged(d),
        out_shape=jax.ShapeDtypeStruct((1, tg.n_tok, d), F32),
        scratch_shapes=scratch,
        compiler_params=_cparams(1, 56),
        name=name,
    )(*args)


POOL_PAD = SUBLANES * 2


def _pool_kernel(p_ref, w_ref, s_ref, o_ref, pad_ref):
    t = p_ref.shape[1]
    gw = w_ref.shape[-1]
    zeros = jnp.zeros((POOL_PAD, gw), F32)
    pad_ref[0:POOL_PAD, :] = zeros

    def count(row0, n_rows, half):
        pos = row0 + lax.broadcasted_iota(jnp.int32, (n_rows, gw), 0)
        return (jnp.minimum(pos + half, t) - jnp.maximum(pos - half, 0)).astype(F32)

    for gi, win in enumerate(POOL_WINDOWS):
        half = win // 2
        u = p_ref[0, :, gi * gw:(gi + 1) * gw].astype(F32)
        ext = t + POOL_PAD
        acc = jnp.concatenate([u, zeros], axis=0)
        span = 1
        while span < win:
            pad_ref[POOL_PAD:POOL_PAD + ext, :] = acc
            acc = acc + pad_ref[POOL_PAD - span:POOL_PAD - span + ext, :]
            span *= 2
        pad_ref[POOL_PAD:POOL_PAD + ext, :] = acc
        wsum = pad_ref[POOL_PAD + half - 1:POOL_PAD + half - 1 + t, :]
        if win & (win - 1) == 0 and half <= SUBLANES <= t // 2:
            head, body, tail = slice(0, SUBLANES), slice(SUBLANES, t - SUBLANES), slice(t - SUBLANES, t)
            y = jnp.concatenate([wsum[head] / count(0, SUBLANES, half) - u[head],
                                 wsum[body] * (1.0 / win) - u[body],
                                 wsum[tail] / count(t - SUBLANES, SUBLANES, half) - u[tail]], axis=0)
        else:
            y = wsum / count(0, t, half) - u
        y = y.astype(BF16)
        z = jnp.dot(y, w_ref[gi], preferred_element_type=F32)
        o_ref[0, :, gi * gw:(gi + 1) * gw] = (z * s_ref[:, gi * gw:(gi + 1) * gw]).astype(o_ref.dtype)


def _pool(p, pool_w, pool_scale, n, t, first_block, name):
    w = p.shape[-1]
    gw = pool_w.shape[-1]
    return pl.pallas_call(
        _pool_kernel,
        grid=(n,),
        in_specs=[
            pl.BlockSpec((1, t, w), lambda b: (0, first_block + b, 0)),
            _resident(pool_w.shape), _resident(pool_scale.shape),
        ],
        out_specs=pl.BlockSpec((1, t, w), lambda b: (0, b, 0)),
        out_shape=jax.ShapeDtypeStruct((1, n * t, w), BF16),
        scratch_shapes=[pltpu.VMEM((t + 2 * POOL_PAD, gw), F32)],
        compiler_params=_cparams(1, 48),
        name=name,
    )(p, pool_w, pool_scale)


def _head_block_mask(shape):
    r = lax.broadcasted_iota(jnp.int32, shape, 0) // NA_HEAD_DIM
    c = lax.broadcasted_iota(jnp.int32, shape, 1) // NA_HEAD_DIM
    return r == c


def _attend(q_blk, key_parts, val_parts, bias_parts):
    nq, g = q_blk.shape
    heads = g // NA_HEAD_DIM
    qbd = jnp.where(_head_block_mask((heads * nq, g)),
                    jnp.concatenate([q_blk] * heads, axis=0), jnp.zeros((), BF16))
    nt = (((1,), (1,)), ((), ()))
    scores = []
    for keys, bias in zip(key_parts, bias_parts):
        s = lax.dot_general(keys, qbd, nt, preferred_element_type=F32)
        scores.append(s if bias is None else s + bias)
    m = functools.reduce(jnp.maximum, [jnp.max(s, axis=0, keepdims=True) for s in scores])
    ps = [jnp.exp2(s - m) for s in scores]
    l = functools.reduce(jnp.add, [jnp.sum(p, axis=0, keepdims=True) for p in ps])
    tn = (((0,), (0,)), ((), ()))
    r = None
    for p, vals in zip(ps, val_parts):
        part = lax.dot_general(p.astype(BF16), vals, tn, preferred_element_type=F32)
        r = part if r is None else r + part
    inv_t = jnp.transpose(jnp.broadcast_to(1.0 / l, (LANES, heads * nq)))
    inv_rows = jnp.concatenate([inv_t] * (g // LANES), axis=1)
    lane_head = lax.broadcasted_iota(jnp.int32, (nq, g), 1) // NA_HEAD_DIM
    out = jnp.zeros((nq, g), F32)
    for h in range(heads):
        rows = slice(h * nq, (h + 1) * nq)
        out = jnp.where(lane_head == h, r[rows, :] * inv_rows[rows, :], out)
    return out


NATTN_UNROLL = 32


def _nattn_kernel(q_ref, k_ref, v_ref, qc_ref, kc_ref, vc_ref, bias_ref, o_ref, oc_ref):
    s = q_ref.shape[1]
    rows = s // GRID_W
    kh = min(NB_ROWS, rows)
    kc = kc_ref[0]
    vc = vc_ref[0]

    def body(i, carry):
        r0 = jnp.clip(i - kh // 2, 0, rows - kh)
        q0 = pl.multiple_of(i * GRID_W, GRID_W)
        k0 = pl.multiple_of(r0 * GRID_W, GRID_W)
        q_blk = q_ref[0, pl.ds(q0, GRID_W), :]
        keys = k_ref[0, pl.ds(k0, kh * GRID_W), :]
        vals = v_ref[0, pl.ds(k0, kh * GRID_W), :]
        bias = bias_ref[0, pl.ds(r0 - i + (NB_ROWS - 1), kh), :, :]
        bias = bias.reshape(kh * GRID_W, bias.shape[-1])
        out = _attend(q_blk, [keys, kc], [vals, vc], [bias, None])
        o_ref[0, pl.ds(q0, GRID_W), :] = out.astype(o_ref.dtype)
        return carry

    lax.fori_loop(0, rows, body, 0, unroll=NATTN_UNROLL)

    for i in range(qc_ref.shape[1] // GRID_W):
        blk = slice(i * GRID_W, (i + 1) * GRID_W)
        oc_ref[0, blk, :] = _attend(qc_ref[0, blk, :], [kc], [vc], [None]).astype(oc_ref.dtype)


def _bias_table(rpb):
    h = rpb.shape[0]
    kcol = np.arange(GRID_W)[:, None]
    qcol = np.arange(GRID_W)[None, :]
    qstart = np.clip(qcol - NB_COLS // 2, 0, GRID_W - NB_COLS)
    mask = (kcol >= qstart) & (kcol < qstart + NB_COLS)
    idx = np.clip(kcol - qcol + NB_COLS - 1, 0, 2 * NB_COLS - 2)
    onehot = (idx[None] == np.arange(rpb.shape[-1])[:, None, None]).astype(np.float32)
    t = jnp.einsum("hrj,jkq->hrkq", rpb.astype(F32), onehot, precision=lax.Precision.HIGHEST)
    t = jnp.where(mask[None, None], t * LOG2_E, -jnp.inf)
    t = t.reshape(h // HEADS_PER_GROUP, HEADS_PER_GROUP, t.shape[1], GRID_W, GRID_W)
    t = jnp.transpose(t, (0, 2, 3, 1, 4))
    return t.reshape(h // HEADS_PER_GROUP, t.shape[1], GRID_W, HEADS_PER_GROUP * GRID_W)


def _nattn(q, k, v, rpb, b, s, n_ctx):
    w = q.shape[-1]
    g = HEADS_PER_GROUP * NA_HEAD_DIM
    bias = _bias_table(rpb)
    ctx0 = (b * s) // n_ctx
    lat = pl.BlockSpec((1, s, g), lambda i, j: (0, i, j))
    ctx = pl.BlockSpec((1, n_ctx, g), lambda i, j: (0, ctx0 + i, j))
    return pl.pallas_call(
        _nattn_kernel,
        grid=(b, w // g),
        in_specs=[lat, lat, lat, ctx, ctx, ctx,
                  pl.BlockSpec((1,) + bias.shape[1:], lambda i, j: (j, 0, 0, 0))],
        out_specs=[lat, pl.BlockSpec((1, n_ctx, g), lambda i, j: (0, i, j))],
        out_shape=[jax.ShapeDtypeStruct((1, b * s, w), BF16),
                   jax.ShapeDtypeStruct((1, b * n_ctx, w), BF16)],
        compiler_params=_cparams(2, 48),
        name="nattn",
    )(q, k, v, q, k, v, bias)


RG_CHUNK = 128
HALO = SUBLANES
SCAN_UNROLL = RG_CHUNK
COEF_UNROLL = 2


def _rglru_kernel(*refs, n_ctx, latent):
    if latent:
        (rxf_ref, rxf_prev_ref, rxf_next_ref, rxb_ref, rxb_prev_ref, rxb_next_ref, init_ref,
         cw_ref, cb_ref, gw_ref, lam_ref, hf_ref, hb_ref,
         win_ref, af_ref, bf_ref, ab_ref, bb_ref, sf_ref, sb_ref, state_ref) = refs
        nbatch, nblk, tc, _ = rxf_ref.shape
    else:
        (rc_ref, cw_ref, cb_ref, gw_ref, lam_ref, final_ref,
         win_ref, af_ref, bf_ref, ab_ref, bb_ref, state_ref) = refs
        nblk, tc = rc_ref.shape[0], RG_CHUNK
        nbatch = rc_ref.shape[1] // n_ctx
    ngroups = nblk // COEF_UNROLL
    n_chunks = pl.num_programs(0)
    c = pl.program_id(0)

    @pl.when(c == 0)
    def _():
        state_ref[...] = init_ref[...] if latent else jnp.zeros(state_ref.shape, F32)

    lane = lax.broadcasted_iota(jnp.int32, (nbatch * tc, RG_BW), 1)
    bias_lanes = (lane < 2).astype(F32).astype(BF16)

    def fill_from_ctx(win, n, j):
        for b in range(nbatch):
            base = b * n_ctx + j * tc
            prev0 = pl.multiple_of(jnp.maximum(base - HALO, b * n_ctx), HALO)
            next0 = pl.multiple_of(jnp.minimum(base + tc, (b + 1) * n_ctx - HALO), HALO)
            win[b, 0:HALO, :] = jnp.where(j > 0, rc_ref[n, pl.ds(prev0, HALO), :], 0.0)
            win[b, HALO:HALO + tc, :] = rc_ref[n, pl.ds(pl.multiple_of(base, tc), tc), :]
            win[b, HALO + tc:2 * HALO + tc, :] = jnp.where(
                j < n_chunks - 1, rc_ref[n, pl.ds(next0, HALO), :], 0.0)

    def fill_from_x(win, n, j, cur_ref, prev_ref, next_ref):
        for b in range(nbatch):
            win[b, 0:HALO, :] = jnp.where(j > 0, prev_ref[b, n], 0.0)
            win[b, HALO:HALO + tc, :] = cur_ref[b, n]
            win[b, HALO + tc:2 * HALO + tc, :] = jnp.where(j < n_chunks - 1, next_ref[b, n], 0.0)

    def coefficients(d, j, x_refs, a_ref, b_ref):
        cols = slice(d * 2 * RG_BW, (d + 1) * 2 * RG_BW)

        def fill(n, win):
            if latent:
                fill_from_x(win, n, j, *x_refs)
            else:
                fill_from_ctx(win, n, j)

        def body(g, p, win):
            n = g * COEF_UNROLL + p
            cw = cw_ref[n]
            hus = []
            for b in range(nbatch):
                y = cb_ref[n]
                for k in range(CONV_W):
                    lo = HALO + k - CONV_W // 2
                    y = y + win[b, lo:lo + tc, :] * cw[k:k + 1, :]
                hus.append(y)
            hu = jnp.concatenate(hus, axis=0)
            lhs = jnp.concatenate([hu.astype(BF16), bias_lanes], axis=1)
            z = jnp.dot(lhs, gw_ref[n, :, cols], preferred_element_type=F32)
            t_r = jnp.tanh(z[:, :RG_BW])
            t_i = jnp.tanh(z[:, RG_BW:])
            lam = -lam_ref[n, d:d + 1, :]
            softplus = jnp.maximum(lam, 0.0) + jnp.log1p(jnp.exp(-jnp.abs(lam)))
            log_a = (t_r + 1.0) * ((-0.5 * RG_C) * softplus)
            a = jnp.exp(log_a)
            th = jnp.tanh(log_a)
            num = -2.0 * th
            root = jnp.where(num > 0.0, num * lax.rsqrt(num * (1.0 - th)), 0.0)
            coef = root * ((t_i + 1.0) * hu)
            for b in range(nbatch):
                rows = pl.ds(p * nbatch + b, tc, stride=SUBLANES)
                a_ref[g, rows, :] = a[b * tc:(b + 1) * tc]
                b_ref[g, rows, :] = coef[b * tc:(b + 1) * tc]

        return fill, body

    x_refs_f = (rxf_ref, rxf_prev_ref, rxf_next_ref) if latent else None
    x_refs_b = (rxb_ref, rxb_prev_ref, rxb_next_ref) if latent else None
    stages = [coefficients(0, c, x_refs_f, af_ref, bf_ref),
              coefficients(1, n_chunks - 1 - c, x_refs_b, ab_ref, bb_ref)]

    def group(g, carry):
        for d, (fill, _) in enumerate(stages):
            for p in range(COEF_UNROLL):
                fill(g * COEF_UNROLL + p, win_ref.at[d * COEF_UNROLL + p])
        for d, (_, body) in enumerate(stages):
            for p in range(COEF_UNROLL):
                body(g, p, win_ref.at[d * COEF_UNROLL + p])
        return carry

    lax.fori_loop(0, ngroups, group, 0, unroll=True)

    def scan_step(t, hs):
        out_f, out_b = [], []
        rows_f = pl.ds(pl.multiple_of(t * SUBLANES, SUBLANES), SUBLANES)
        rows_b = pl.ds(pl.multiple_of((tc - 1 - t) * SUBLANES, SUBLANES), SUBLANES)
        for g in range(ngroups):
            h = af_ref[g, rows_f, :] * hs[g] + bf_ref[g, rows_f, :]
            if latent:
                sf_ref[g, rows_f, :] = h
            out_f.append(h)
            h = ab_ref[g, rows_b, :] * hs[ngroups + g] + bb_ref[g, rows_b, :]
            if latent:
                sb_ref[g, rows_b, :] = h
            out_b.append(h)
        return tuple(out_f + out_b)

    hs = tuple(state_ref[d, g] for d in range(2) for g in range(ngroups))
    hs = lax.fori_loop(0, tc, scan_step, hs, unroll=SCAN_UNROLL)
    for d in range(2):
        for g in range(ngroups):
            state_ref[d, g] = hs[d * ngroups + g]

    if not latent:
        final_ref[...] = state_ref[...]
        return

    def emit(g, carry):
        for p in range(COEF_UNROLL):
            for b in range(nbatch):
                rows = pl.ds(p * nbatch + b, tc, stride=SUBLANES)
                hf_ref[b, g * COEF_UNROLL + p] = sf_ref[g, rows, :].astype(hf_ref.dtype)
                hb_ref[b, g * COEF_UNROLL + p] = sb_ref[g, rows, :].astype(hb_ref.dtype)
        return carry
    lax.fori_loop(0, ngroups, emit, 0, unroll=True)


def _rglru(rx, rc, n_ctx, conv_w, conv_b, gate_w, gate_b, lam):
    b, nb, s, _ = rx.shape
    tc = RG_CHUNK
    ctx_chunks = n_ctx // tc
    x_chunks = s // tc
    per_halo = tc // HALO
    gw = jnp.transpose(gate_w, (2, 3, 0, 1, 4)).reshape(nb, RG_BW, 4 * RG_BW).astype(BF16)
    gb = 0.5 * jnp.transpose(gate_b.reshape(2, 2, nb, RG_BW), (2, 0, 1, 3)).reshape(nb, 1, 4 * RG_BW)
    gb_hi = gb.astype(BF16)
    gb_lo = (gb - gb_hi.astype(F32)).astype(BF16)
    pad = jnp.zeros((nb, MXU_DIM - RG_BW - 2, 4 * RG_BW), BF16)
    gw = jnp.concatenate([gw, gb_hi, gb_lo, pad], axis=1)
    cw = 0.5 * jnp.transpose(conv_w.reshape(CONV_W, nb, RG_BW), (1, 0, 2))
    cb = 0.5 * conv_b.reshape(nb, 1, RG_BW)
    lam_s = jnp.transpose(lam.reshape(2, nb, RG_BW), (1, 0, 2))

    assert COEF_UNROLL * b == SUBLANES and nb % COEF_UNROLL == 0, (b, nb)
    ngroups = nb // COEF_UNROLL
    slab = pltpu.VMEM((ngroups, tc * SUBLANES, LANES), F32)
    window = pltpu.VMEM((2 * COEF_UNROLL, b, tc + 2 * HALO, LANES), F32)
    carry = jax.ShapeDtypeStruct((2, ngroups, SUBLANES, LANES), F32)
    carry_scratch = pltpu.VMEM(carry.shape, F32)
    params = [cw, cb, gw, lam_s]
    param_specs = [_resident(a.shape) for a in params]

    ctx_state = pl.pallas_call(
        functools.partial(_rglru_kernel, n_ctx=n_ctx, latent=False),
        grid=(ctx_chunks,),
        in_specs=[_resident(rc.shape)] + param_specs,
        out_specs=pl.BlockSpec(carry.shape, lambda c: (0, 0, 0, 0)),
        out_shape=carry,
        scratch_shapes=[window, slab, slab, slab, slab, carry_scratch],
        compiler_params=_cparams(1, 32),
        name="rglru_ctx",
    )(rc, *params)

    chunk_f = lambda c: c
    chunk_b = lambda c: x_chunks - 1 - c
    cur = lambda f: pl.BlockSpec((b, nb, tc, LANES), lambda c: (0, 0, f(c), 0))
    prev = lambda f: pl.BlockSpec((b, nb, HALO, LANES),
                                  lambda c: (0, 0, jnp.maximum(f(c) * per_halo - 1, 0), 0))
    nxt = lambda f: pl.BlockSpec((b, nb, HALO, LANES),
                                 lambda c: (0, 0, jnp.minimum((f(c) + 1) * per_halo, s // HALO - 1), 0))
    state_shape = jax.ShapeDtypeStruct((b, nb, s, LANES), BF16)
    return pl.pallas_call(
        functools.partial(_rglru_kernel, n_ctx=n_ctx, latent=True),
        grid=(x_chunks,),
        in_specs=[cur(chunk_f), prev(chunk_f), nxt(chunk_f), cur(chunk_b), prev(chunk_b), nxt(chunk_b),
                  _resident(carry.shape)] + param_specs,
        out_specs=[cur(chunk_f), cur(chunk_b)],
        out_shape=[state_shape, state_shape],
        scratch_shapes=[window, slab, slab, slab, slab, slab, slab, carry_scratch],
        compiler_params=_cparams(1, 52),
        name="rglru",
    )(rx, rx, rx, rx, rx, rx, ctx_state, *params)


def kernel(x, c, ctx, c_ctx, ada_w, ada_b, norm_g, ffn_w_gu, ffn_w_down, ab_w_in, ab_w_out, pool_w,
           pool_scale, na_rpb, rg_w_in, rg_conv_w, rg_conv_b, rg_gate_w, rg_gate_b, rg_lambda,
           rg_w_out, final_g):
    b, s, d = x.shape
    n_ctx = ctx.shape[1]
    depth = ada_w.shape[0]
    pool_width = pool_w.shape[1] * pool_w.shape[2]
    na_width = (ab_w_in.shape[-1] - pool_width) // 3
    d_rnn = rg_w_out.shape[1]
    na_scale = NA_HEAD_DIM ** -0.5 * LOG2_E

    mods = _mods(c, c_ctx, ada_w, ada_b)
    tg = _TokenGrid(b, s, n_ctx)
    tg_x = _TokenGrid(b, s, n_ctx, with_ctx=False)
    xs = [x.reshape(1, b * s, d), ctx.reshape(1, b * n_ctx, d)]

    for i in range(depth):
        last = i == depth - 1
        mod = mods[i, :b + 1]
        g = norm_g[i]
        j = i // 2
        fg = final_g.reshape(1, d) if last else None
        if i % 2 == 0:
            pw = pool_w[j].astype(BF16)
            ps = pool_scale[j].reshape(1, pool_width)
            splits = ((pool_width, 1.0, False), (na_width, na_scale, False), (na_width, 1.0, False),
                      (na_width, 1.0, False))
            dts = (BF16, BF16, BF16, BF16)
            xm, p, q, k, v = _pre(tg, xs, mod, g, ffn_w_gu, ffn_w_down, ab_w_in, (i, j), splits, dts,
                                  f"pre{i}")
            a_x = _pool(p, pw, ps, b, s, 0, f"pool{i}_x")
            b_x, b_c = _nattn(q, k, v, na_rpb[j], b, s, n_ctx)
            if last:
                raise NotImplementedError("a final pooling/attention layer is not needed at this depth")
            a_c = _pool(p, pw, ps, b, n_ctx, (b * s) // n_ctx, f"pool{i}_c")
            xm = _post(tg, xm, mod, g, [(a_x, a_c), (b_x, b_c)], ab_w_out, ffn_w_gu, ffn_w_down, (i, j),
                       fg, f"post{i}")
        else:
            splits = ((d_rnn, 1.0, False), (d_rnn, 1.0, True))
            dts = (BF16, F32)
            xm, gate, (rx, rc) = _pre(tg, xs, mod, g, ffn_w_gu, ffn_w_down, rg_w_in, (i, j), splits, dts,
                                      f"pre{i}")
            hf, hb = _rglru(rx, rc[0], n_ctx, rg_conv_w[j], rg_conv_b[j], rg_gate_w[j], rg_gate_b[j],
                            rg_lambda[j])
            if not last:
                raise NotImplementedError("context output of an RG-LRU layer is not needed at this depth")
            xm = _post(tg_x, xm, mod, g, (gate, hf, hb), rg_w_out, ffn_w_gu, ffn_w_down, (i, j), fg,
                       f"post{i}", gated_scan=True)
        xs = [xm]
    return xm.reshape(b, s, d)
```
